```python
import math
import jax, jax.numpy as jnp
from jax import lax
import numpy as np

D_MODEL = 1024
BATCH = 32
SEQ = 2048
DEPTH = 4

N_MEM = 256
N_MIXERS = 3
N_CONV = (DEPTH + 2) // 3
N_SSD = (DEPTH + 1) // 3
N_DIFF = DEPTH // 3
DN_ALPHA = (2.0 * DEPTH) ** 0.25
DN_BETA = (8.0 * DEPTH) ** -0.25
LN_EPS = 1e-5
CONV_WIDTH = 31
MB_D_INNER = 2 * D_MODEL
MB_HEAD_DIM = 64
MB_HEADS = MB_D_INNER // MB_HEAD_DIM
MB_D_STATE = 128
MB_GROUPS = 8
MB_CONV = 4
MB_CHUNK = 128
MB_XBC = MB_D_INNER + 2 * MB_GROUPS * MB_D_STATE
MB_IN = MB_D_INNER + MB_XBC + MB_HEADS
DA_HEAD_DIM = 64
DA_HEADS = D_MODEL // (2 * DA_HEAD_DIM)
DA_V_DIM = 2 * DA_HEAD_DIM
ROPE_THETA = 500000.0
ROPE_DIM = DA_HEAD_DIM // 4
Q_BLOCK = 128
XA_HEADS = 4
XA_HEAD_DIM = D_MODEL // XA_HEADS
D_FF = ((8 * D_MODEL + 3 * 256 - 1) // (3 * 256)) * 256

kernel_name = "hybrid_conv_ssd_diffattn_deepnorm"


def layer_norm(x, g, b):
    xf = x.astype(jnp.float32)
    mu = jnp.mean(xf, -1, keepdims=True)
    var = jnp.mean(jnp.square(xf - mu), -1, keepdims=True)
    return ((xf - mu) * lax.rsqrt(var + LN_EPS) * g.astype(jnp.float32) + b.astype(jnp.float32)).astype(x.dtype)


def rms_norm(x, g, eps=1e-5):
    xf = x.astype(jnp.float32)
    y = xf * lax.rsqrt(jnp.mean(jnp.square(xf), -1, keepdims=True) + eps)
    return (y * g.astype(jnp.float32)).astype(x.dtype)


def post_norm(x, y, g, b):
    return layer_norm(DN_ALPHA * x + y.astype(x.dtype), g, b)


def causal_depthwise_conv(x, w, b):
    k, c = w.shape
    y = lax.conv_general_dilated(
        x, w[:, None, :].astype(x.dtype), window_strides=(1,), padding=[(k - 1, 0)],
        dimension_numbers=('NWC', 'WIO', 'NWC'), feature_group_count=c)
    return y + b.astype(x.dtype)


def conformer_conv(x, w_in, b_in, w_dw, b_dw, ln_g, ln_b, w_out, b_out):
    h = x @ w_in + b_in
    a, gate = jnp.split(h, 2, axis=-1)
    h = a * jax.nn.sigmoid(gate)
    h = causal_depthwise_conv(h, w_dw, b_dw)
    h = jax.nn.silu(layer_norm(h, ln_g, ln_b))
    return h @ w_out + b_out


def ssd_chunked_scan(xs, dt, a, bm, cm):
    b, s, g, hg, p = xs.shape
    n = bm.shape[-1]
    L = MB_CHUNK
    nc = s // L

    def to_chunks(t):
        return jnp.moveaxis(t.reshape((b, nc, L) + t.shape[2:]), 1, 0)

    causal = jnp.tril(jnp.ones((L, L), dtype=bool))

    def step(h, inp):
        xc, dtc, bc, cc = inp
        acum = jnp.cumsum(dtc * a, axis=1)
        seg = acum[:, :, None] - acum[:, None, :]
        decay = jnp.exp(jnp.where(causal[None, :, :, None, None], seg, -jnp.inf))
        cb = jnp.einsum('blgn,bsgn->blsg', cc, bc)
        wts = cb[..., None] * decay * dtc[:, None]
        y_diag = jnp.einsum('blsgh,bsghp->blghp', wts, xc)
        y_off = jnp.einsum('blgn,bghpn->blghp', cc, h) * jnp.exp(acum)[..., None]
        xw = xc * (jnp.exp(acum[:, -1:] - acum) * dtc)[..., None]
        h_new = h * jnp.exp(acum[:, -1])[..., None, None] + jnp.einsum('bsgn,bsghp->bghpn', bc, xw)
        return h_new, y_diag + y_off

    h0 = jnp.zeros((b, g, hg, p, n), jnp.float32)
    _, ys = lax.scan(step, h0, (to_chunks(xs), to_chunks(dt), to_chunks(bm), to_chunks(cm)))
    return jnp.moveaxis(ys, 0, 1).reshape(b, s, g, hg, p)


def ssd_mixer(x, w_in, w_conv, b_conv, dt_bias, a_log, d_skip, norm_g, w_out):
    b, s, _ = x.shape
    f32 = jnp.float32
    hpg = MB_HEADS // MB_GROUPS
    zxbcdt = x @ w_in
    z, xbc, dt = jnp.split(zxbcdt, [MB_D_INNER, MB_D_INNER + MB_XBC], axis=-1)
    xbc = jax.nn.silu(causal_depthwise_conv(xbc, w_conv, b_conv))
    xs, bm, cm = jnp.split(xbc, [MB_D_INNER, MB_D_INNER + MB_GROUPS * MB_D_STATE], axis=-1)
    xs = xs.reshape(b, s, MB_GROUPS, hpg, MB_HEAD_DIM).astype(f32)
    bm = bm.reshape(b, s, MB_GROUPS, MB_D_STATE).astype(f32)
    cm = cm.reshape(b, s, MB_GROUPS, MB_D_STATE).astype(f32)
    dt = jax.nn.softplus(dt.astype(f32) + dt_bias.astype(f32)).reshape(b, s, MB_GROUPS, hpg)
    a = -jnp.exp(a_log.astype(f32)).reshape(MB_GROUPS, hpg)
    y = ssd_chunked_scan(xs, dt, a, bm, cm)
    y = y + d_skip.astype(f32).reshape(MB_GROUPS, hpg)[..., None] * xs
    y = y.reshape(b, s, MB_D_INNER) * jax.nn.silu(z.astype(f32))
    y = y.reshape(b, s, MB_GROUPS, MB_D_INNER // MB_GROUPS)
    y = y * lax.rsqrt(jnp.mean(jnp.square(y), -1, keepdims=True) + 1e-5)
    y = y.reshape(b, s, MB_D_INNER) * norm_g.astype(f32)
    return y.astype(x.dtype) @ w_out


def rope_tables(positions):
    inv = ROPE_THETA ** (-jnp.arange(0, ROPE_DIM, 2, dtype=jnp.float32) / ROPE_DIM)
    ang = positions.astype(jnp.float32)[..., None] * inv
    return jnp.cos(ang)[:, :, None, :], jnp.sin(ang)[:, :, None, :]


def rope_partial(t, cos, sin):
    half = ROPE_DIM // 2
    t1, t2, tp = t[..., :half], t[..., half:ROPE_DIM], t[..., ROPE_DIM:]
    return jnp.concatenate([t1 * cos - t2 * sin, t2 * cos + t1 * sin, tp.astype(cos.dtype)], axis=-1).astype(t.dtype)


def diff_attention(x, cos, sin, w_qkv, lq1, lk1, lq2, lk2, subln_g, w_out, lambda_init):
    b, s, _ = x.shape
    f32 = jnp.float32
    q, k, v = jnp.split(x @ w_qkv, 3, axis=-1)
    q = rope_partial(q.reshape(b, s, 2 * DA_HEADS, DA_HEAD_DIM), cos, sin)
    k = rope_partial(k.reshape(b, s, 2 * DA_HEADS, DA_HEAD_DIM), cos, sin)
    q = q.reshape(b, s, DA_HEADS, 2, DA_HEAD_DIM) * (DA_HEAD_DIM ** -0.5)
    k = k.reshape(b, s, DA_HEADS, 2, DA_HEAD_DIM)
    v = v.reshape(b, s, DA_HEADS, DA_V_DIM)
    lam = (jnp.exp(jnp.sum(lq1.astype(f32) * lk1.astype(f32)))
           - jnp.exp(jnp.sum(lq2.astype(f32) * lk2.astype(f32))) + lambda_init)
    outs = []
    for i in range(s // Q_BLOCK):
        q0 = i * Q_BLOCK
        kend = q0 + Q_BLOCK
        sc = jnp.einsum('bqhcd,bkhcd->bhcqk', q[:, q0:kend], k[:, :kend]).astype(f32)
        mask = (q0 + jnp.arange(Q_BLOCK))[:, None] >= jnp.arange(kend)[None, :]
        p = jax.nn.softmax(jnp.where(mask, sc, -jnp.inf), axis=-1)
        pd = p[:, :, 0] - lam * p[:, :, 1]
        outs.append(jnp.einsum('bhqk,bkhe->bqhe', pd.astype(v.dtype), v[:, :kend]))
    o = jnp.concatenate(outs, axis=1)
    o = rms_norm(o, subln_g) * (1.0 - lambda_init)
    return o.reshape(b, s, D_MODEL) @ w_out


def memory_cross_attention(x, mem, w_q, w_kv, w_out):
    b, s, _ = x.shape
    m = mem.shape[1]
    q = (x @ w_q).reshape(b, s, XA_HEADS, XA_HEAD_DIM) * (XA_HEAD_DIM ** -0.5)
    k, v = jnp.split(mem @ w_kv, 2, axis=-1)
    k = k.reshape(b, m, XA_HEADS, XA_HEAD_DIM)
    v = v.reshape(b, m, XA_HEADS, XA_HEAD_DIM)
    sc = jnp.einsum('bshd,bmhd->bhsm', q, k).astype(jnp.float32)
    p = jax.nn.softmax(sc, axis=-1)
    o = jnp.einsum('bhsm,bmhd->bshd', p.astype(v.dtype), v)
    return o.reshape(b, s, D_MODEL) @ w_out


def swiglu(x, w_in, w_out):
    g, u = jnp.split(x @ w_in, 2, axis=-1)
    return (jax.nn.silu(g) * u) @ w_out


def setup_inputs(seed: int = 0) -> dict:
    key = jax.random.key(seed)
    f32 = jnp.float32
    counter = [0]

    def nxt():
        counter[0] += 1
        return jax.random.fold_in(key, counter[0])

    def nrm(shape, scale):
        return jax.random.normal(nxt(), shape, f32) * scale

    D = D_MODEL
    x = nrm((BATCH, SEQ, D), 1.0)
    mem = nrm((BATCH, N_MEM, D), 1.0)
    offs = jax.random.randint(nxt(), (BATCH, 1), 0, 1024, dtype=jnp.int32)
    positions = (jnp.arange(SEQ, dtype=jnp.int32)[None, :] + offs).astype(jnp.int32)
    cv_w_in = nrm((N_CONV, D, 2 * D), D ** -0.5)
    cv_b_in = nrm((N_CONV, 2 * D), 0.02)
    cv_w_dw = nrm((N_CONV, CONV_WIDTH, D), CONV_WIDTH ** -0.5)
    cv_b_dw = nrm((N_CONV, D), 0.02)
    cv_ln_g = 1.0 + nrm((N_CONV, D), 0.02)
    cv_ln_b = nrm((N_CONV, D), 0.02)
    cv_w_out = nrm((N_CONV, D, D), D ** -0.5 * DN_BETA)
    cv_b_out = nrm((N_CONV, D), 0.02)
    mb_w_in = nrm((N_SSD, D, MB_IN), D ** -0.5)
    mb_w_conv = nrm((N_SSD, MB_CONV, MB_XBC), MB_CONV ** -0.5)
    mb_b_conv = nrm((N_SSD, MB_XBC), 0.02)
    dt0 = jnp.exp(jax.random.uniform(nxt(), (N_SSD, MB_HEADS), f32, math.log(1e-3), math.log(1e-1)))
    mb_dt_bias = dt0 + jnp.log(-jnp.expm1(-dt0))
    mb_a_log = jnp.log(jax.random.uniform(nxt(), (N_SSD, MB_HEADS), f32, 1.0, 16.0))
    mb_d = 1.0 + nrm((N_SSD, MB_HEADS), 0.02)
    mb_norm_g = 1.0 + nrm((N_SSD, MB_D_INNER), 0.02)
    mb_w_out = nrm((N_SSD, MB_D_INNER, D), MB_D_INNER ** -0.5 * DN_BETA)
    da_w_qkv = nrm((N_DIFF, D, 3 * D), D ** -0.5)
    da_lq1 = nrm((N_DIFF, DA_HEAD_DIM), 0.1)
    da_lk1 = nrm((N_DIFF, DA_HEAD_DIM), 0.1)
    da_lq2 = nrm((N_DIFF, DA_HEAD_DIM), 0.1)
    da_lk2 = nrm((N_DIFF, DA_HEAD_DIM), 0.1)
    da_subln_g = 1.0 + nrm((N_DIFF, DA_V_DIM), 0.02)
    da_w_out = nrm((N_DIFF, D, D), D ** -0.5 * DN_BETA)
    xa_w_q = nrm((DEPTH, D, D), D ** -0.5)
    xa_w_kv = nrm((DEPTH, D, 2 * D), D ** -0.5)
    xa_w_out = nrm((DEPTH, D, D), D ** -0.5 * DN_BETA)
    ff_w_in = nrm((DEPTH, D, 2 * D_FF), D ** -0.5)
    ff_w_out = nrm((DEPTH, D_FF, D), D_FF ** -0.5 * DN_BETA)
    ln_g = 1.0 + nrm((DEPTH, 3, D), 0.02)
    ln_b = nrm((DEPTH, 3, D), 0.02)
    return {
        "x": x, "mem": mem, "positions": positions,
        "cv_w_in": cv_w_in, "cv_b_in": cv_b_in, "cv_w_dw": cv_w_dw, "cv_b_dw": cv_b_dw,
        "cv_ln_g": cv_ln_g, "cv_ln_b": cv_ln_b, "cv_w_out": cv_w_out, "cv_b_out": cv_b_out,
        "mb_w_in": mb_w_in, "mb_w_conv": mb_w_conv, "mb_b_conv": mb_b_conv,
        "mb_dt_bias": mb_dt_bias, "mb_a_log": mb_a_log, "mb_d": mb_d,
        "mb_norm_g": mb_norm_g, "mb_w_out": mb_w_out,
        "da_w_qkv": da_w_qkv, "da_lq1": da_lq1, "da_lk1": da_lk1, "da_lq2": da_lq2,
        "da_lk2": da_lk2, "da_subln_g": da_subln_g, "da_w_out": da_w_out,
        "xa_w_q": xa_w_q, "xa_w_kv": xa_w_kv, "xa_w_out": xa_w_out,
        "ff_w_in": ff_w_in, "ff_w_out": ff_w_out,
        "ln_g": ln_g, "ln_b": ln_b,
    }


def reference(x, mem, positions,
              cv_w_in, cv_b_in, cv_w_dw, cv_b_dw, cv_ln_g, cv_ln_b, cv_w_out, cv_b_out,
              mb_w_in, mb_w_conv, mb_b_conv, mb_dt_bias, mb_a_log, mb_d, mb_norm_g, mb_w_out,
              da_w_qkv, da_lq1, da_lk1, da_lq2, da_lk2, da_subln_g, da_w_out,
              xa_w_q, xa_w_kv, xa_w_out,
              ff_w_in, ff_w_out,
              ln_g, ln_b):
    cos, sin = rope_tables(positions)
    for i in range(DEPTH):
        mixer, j = i % N_MIXERS, i // N_MIXERS
        if mixer == 0:
            y = conformer_conv(x, cv_w_in[j], cv_b_in[j], cv_w_dw[j], cv_b_dw[j],
                               cv_ln_g[j], cv_ln_b[j], cv_w_out[j], cv_b_out[j])
        elif mixer == 1:
            y = ssd_mixer(x, mb_w_in[j], mb_w_conv[j], mb_b_conv[j], mb_dt_bias[j],
                          mb_a_log[j], mb_d[j], mb_norm_g[j], mb_w_out[j])
        else:
            lambda_init = 0.8 - 0.6 * math.exp(-0.3 * i)
            y = diff_attention(x, cos, sin, da_w_qkv[j], da_lq1[j], da_lk1[j], da_lq2[j],
                               da_lk2[j], da_subln_g[j], da_w_out[j], lambda_init)
        x = post_norm(x, y, ln_g[i, 0], ln_b[i, 0])
        x = post_norm(x, memory_cross_attention(x, mem, xa_w_q[i], xa_w_kv[i], xa_w_out[i]),
                      ln_g[i, 1], ln_b[i, 1])
        x = post_norm(x, swiglu(x, ff_w_in[i], ff_w_out[i]), ln_g[i, 2], ln_b[i, 2])
    return x
```

```python
import functools
import math

import jax
import jax.numpy as jnp
from jax import lax
from jax.experimental import pallas as pl
from jax.experimental.pallas import tpu as pltpu

F32 = jnp.float32
BF16 = jnp.bfloat16

D_MODEL = 1024
DEPTH = 4
N_MIXERS = 3
DN_ALPHA = (2.0 * DEPTH) ** 0.25
LN_EPS = 1e-5
CONV_WIDTH = 31
MB_D_INNER = 2 * D_MODEL
MB_HEAD_DIM = 64
MB_HEADS = MB_D_INNER // MB_HEAD_DIM
MB_D_STATE = 128
MB_GROUPS = 8
MB_HPG = MB_HEADS // MB_GROUPS
MB_GROUP_DIM = MB_D_INNER // MB_GROUPS
MB_CONV = 4
MB_CHUNK = 128
MB_XBC = MB_D_INNER + 2 * MB_GROUPS * MB_D_STATE
DA_HEAD_DIM = 64
DA_HEADS = D_MODEL // (2 * DA_HEAD_DIM)
DA_V_DIM = 2 * DA_HEAD_DIM
ROPE_THETA = 500000.0
ROPE_DIM = DA_HEAD_DIM // 4
XA_HEADS = 4
XA_HEAD_DIM = D_MODEL // XA_HEADS
D_FF = ((8 * D_MODEL + 3 * 256 - 1) // (3 * 256)) * 256

LANES = 128
SUBLANES = 8
VMEM_LIMIT_BYTES = 48 * 1024 * 1024


def _params(n_axes):
    return pltpu.CompilerParams(dimension_semantics=("arbitrary",) * n_axes,
                                vmem_limit_bytes=VMEM_LIMIT_BYTES)


def _row_tile(n_rows, want):
    t = min(n_rows, want)
    assert n_rows % t == 0, (n_rows, t)
    return t


def _dot(a, b):
    return jnp.dot(a, b, preferred_element_type=F32)


def _dot_nt(a, b):
    return lax.dot_general(a, b, (((1,), (1,)), ((), ())), preferred_element_type=F32)


def _layer_norm(r, g, b):
    mu = jnp.mean(r, axis=-1, keepdims=True)
    c = r - mu
    var = jnp.mean(c * c, axis=-1, keepdims=True)
    return c * lax.rsqrt(var + LN_EPS) * g + b


def _silu(v):
    return v * jax.nn.sigmoid(v)


def _mm_kernel(x_ref, w_ref, o_ref):
    o_ref[...] = _dot(x_ref[...], w_ref[...]).astype(o_ref.dtype)


def _mm(x, w, out_dtype, tm, tn):
    t, k = x.shape
    n = w.shape[1]
    tm = _row_tile(t, tm)
    tn = _row_tile(n, tn)
    return pl.pallas_call(
        _mm_kernel,
        grid=(t // tm, n // tn),
        in_specs=[pl.BlockSpec((tm, k), lambda i, j: (i, 0)),
                  pl.BlockSpec((k, tn), lambda i, j: (0, j))],
        out_specs=pl.BlockSpec((tm, tn), lambda i, j: (i, j)),
        out_shape=jax.ShapeDtypeStruct((t, n), out_dtype),
        compiler_params=_params(2),
        name="mm",
    )(x, w)


def _mm_postnorm_kernel(h_ref, w_ref, bias_ref, x_ref, g_ref, b_ref, of_ref, ob_ref):
    y = _dot(h_ref[...], w_ref[...]) + bias_ref[...]
    out = _layer_norm(DN_ALPHA * x_ref[...] + y, g_ref[...], b_ref[...])
    of_ref[...] = out
    ob_ref[...] = out.astype(BF16)


def _mm_postnorm(h, w, bias, x, g, b):
    t, k = h.shape
    tm = _row_tile(t, 512)
    row = lambda i: (i, 0)
    fixed = lambda i: (0, 0)
    return pl.pallas_call(
        _mm_postnorm_kernel,
        grid=(t // tm,),
        in_specs=[pl.BlockSpec((tm, k), row), pl.BlockSpec((k, D_MODEL), fixed),
                  pl.BlockSpec((1, D_MODEL), fixed), pl.BlockSpec((tm, D_MODEL), row),
                  pl.BlockSpec((1, D_MODEL), fixed), pl.BlockSpec((1, D_MODEL), fixed)],
        out_specs=[pl.BlockSpec((tm, D_MODEL), row), pl.BlockSpec((tm, D_MODEL), row)],
        out_shape=[jax.ShapeDtypeStruct((t, D_MODEL), F32),
                   jax.ShapeDtypeStruct((t, D_MODEL), BF16)],
        compiler_params=_params(1),
        name="mm_postnorm",
    )(h, w, bias.reshape(1, D_MODEL), x, g.reshape(1, D_MODEL), b.reshape(1, D_MODEL))


def _glu_kernel(x_ref, w_ref, b_ref, o_ref):
    h = _dot(x_ref[...], w_ref[...]) + b_ref[...]
    o_ref[...] = (h[:, :D_MODEL] * jax.nn.sigmoid(h[:, D_MODEL:])).astype(o_ref.dtype)


def _glu(xb, w, b):
    t = xb.shape[0]
    tm = _row_tile(t, 512)
    return pl.pallas_call(
        _glu_kernel,
        grid=(t // tm,),
        in_specs=[pl.BlockSpec((tm, D_MODEL), lambda i: (i, 0)),
                  pl.BlockSpec((D_MODEL, 2 * D_MODEL), lambda i: (0, 0)),
                  pl.BlockSpec((1, 2 * D_MODEL), lambda i: (0, 0))],
        out_specs=pl.BlockSpec((tm, D_MODEL), lambda i: (i, 0)),
        out_shape=jax.ShapeDtypeStruct((t, D_MODEL), BF16),
        compiler_params=_params(1),
        name="glu",
    )(xb, w, b.reshape(1, 2 * D_MODEL))


DW_HALO = 32
DW_ROWS = 64
DW_COLS = 256


def _dwconv_kernel(u_ref, w_ref, bdw_ref, g_ref, b_ref, o_ref, win_ref, acc_ref, *, ts):
    i = pl.program_id(1)

    @pl.when(i == 0)
    def _():
        win_ref[0:DW_HALO, :] = jnp.zeros((DW_HALO, D_MODEL), F32)

    @pl.when(i > 0)
    def _():
        win_ref[0:DW_HALO, :] = win_ref[ts:ts + DW_HALO, :]

    win_ref[DW_HALO:DW_HALO + ts, :] = u_ref[...].astype(F32)

    first = DW_HALO - (CONV_WIDTH - 1)
    offs = range(first, first + CONV_WIDTH)
    span = [max(o for o in offs if o % SUBLANES == r) - r + DW_ROWS for r in range(SUBLANES)]
    for r0 in range(0, ts, DW_ROWS):
        for c0 in range(0, D_MODEL, DW_COLS):
            shifted = [win_ref[r0 + r:r0 + r + span[r], c0:c0 + DW_COLS]
                       for r in range(SUBLANES)]
            acc = jnp.zeros((DW_ROWS, DW_COLS), F32)
            for k in range(CONV_WIDTH):
                off = first + k
                src = shifted[off % SUBLANES]
                a = (off // SUBLANES) * SUBLANES
                acc = acc + src[a:a + DW_ROWS, :] * w_ref[k:k + 1, c0:c0 + DW_COLS]
            acc_ref[r0:r0 + DW_ROWS, c0:c0 + DW_COLS] = acc

    v = acc_ref[...] + bdw_ref[...]
    o_ref[...] = _silu(_layer_norm(v, g_ref[...], b_ref[...])).astype(o_ref.dtype)


def _dwconv(u, w_dw, b_dw, ln_g, ln_b, batch, seq):
    ts = _row_tile(seq, 256)
    nt = seq // ts
    w_pad = jnp.zeros((DW_HALO, D_MODEL), F32).at[:CONV_WIDTH].set(w_dw)
    fixed = lambda b, i: (0, 0)
    return pl.pallas_call(
        functools.partial(_dwconv_kernel, ts=ts),
        grid=(batch, nt),
        in_specs=[pl.BlockSpec((ts, D_MODEL), lambda b, i: (b * nt + i, 0)),
                  pl.BlockSpec((DW_HALO, D_MODEL), fixed),
                  pl.BlockSpec((1, D_MODEL), fixed), pl.BlockSpec((1, D_MODEL), fixed),
                  pl.BlockSpec((1, D_MODEL), fixed)],
        out_specs=pl.BlockSpec((ts, D_MODEL), lambda b, i: (b * nt + i, 0)),
        out_shape=jax.ShapeDtypeStruct((batch * seq, D_MODEL), BF16),
        scratch_shapes=[pltpu.VMEM((ts + DW_HALO, D_MODEL), F32),
                        pltpu.VMEM((ts, D_MODEL), F32)],
        compiler_params=_params(2),
        name="dwconv",
    )(u, w_pad, b_dw.reshape(1, D_MODEL), ln_g.reshape(1, D_MODEL), ln_b.reshape(1, D_MODEL))


def _split3(v):
    h1 = v.astype(BF16)
    r1 = v - h1.astype(F32)
    h2 = r1.astype(BF16)
    r2 = r1 - h2.astype(F32)
    return h1, h2, r2.astype(BF16)


def _ssd_kernel(xbc_ref, z_ref, dt_ref, wc_ref, bc_ref, dtb_ref, alog_ref, d_ref, ng_ref,
                o_ref, win_ref, xc_ref, st_ref):
    L = MB_CHUNK
    c = pl.program_id(1)

    @pl.when(c == 0)
    def _():
        win_ref[0:SUBLANES, :] = jnp.zeros((SUBLANES, MB_XBC), F32)
        st_ref[...] = jnp.zeros(st_ref.shape, F32)

    @pl.when(c > 0)
    def _():
        win_ref[0:SUBLANES, :] = win_ref[L:L + SUBLANES, :]

    win_ref[SUBLANES:SUBLANES + L, :] = xbc_ref[...].astype(F32)

    cw = 512
    for c0 in range(0, MB_XBC, cw):
        acc = bc_ref[:, c0:c0 + cw] + jnp.zeros((L, cw), F32)
        for k in range(MB_CONV):
            off = SUBLANES - (MB_CONV - 1) + k
            acc = acc + win_ref[off:off + L, c0:c0 + cw] * wc_ref[k:k + 1, c0:c0 + cw]
        xc_ref[:, c0:c0 + cw] = _silu(acc)

    dtr = dt_ref[...] + dtb_ref[...]
    dt = jnp.maximum(dtr, 0.0) + jnp.log1p(jnp.exp(-jnp.abs(dtr)))
    dta = dt * (-jnp.exp(alog_ref[...]))
    dt_t = dt.T
    dta_t = dta.T
    rows = lax.broadcasted_iota(jnp.int32, (L, L), 0)
    cols = lax.broadcasted_iota(jnp.int32, (L, L), 1)
    causal = rows >= cols
    tril = jnp.where(causal, 1.0, 0.0).astype(BF16)
    triu = jnp.where(rows <= cols, 1.0, 0.0).astype(BF16)
    c1, c2, c3 = _split3(dta)
    acol = _dot(tril, c1) + _dot(tril, c2) + _dot(tril, c3)
    r1, r2, r3 = _split3(dta_t)
    arow = _dot(r1, triu) + _dot(r2, triu) + _dot(r3, triu)
    alast = arow[:, L - 1:L]
    state_decay = jnp.broadcast_to(jnp.exp(alast), (LANES, LANES))
    w_in_state = dt_t * jnp.exp(alast - arow)

    lane2 = lax.broadcasted_iota(jnp.int32, (1, MB_GROUP_DIM), 1)
    for g in range(MB_GROUPS):
        x0 = g * MB_GROUP_DIM
        b0 = MB_D_INNER + g * MB_D_STATE
        c0 = MB_D_INNER + MB_GROUPS * MB_D_STATE + g * MB_D_STATE
        bg = xc_ref[:, b0:b0 + MB_D_STATE]
        cg = xc_ref[:, c0:c0 + MB_D_STATE]
        xg = xc_ref[:, x0:x0 + MB_GROUP_DIM]
        xgb = xg.astype(BF16)
        cb = _dot_nt(cg.astype(BF16), bg.astype(BF16))
        bg_t = bg.T
        h_old = st_ref[g]
        h_old_b = h_old.astype(BF16)
        y = jnp.zeros((L, MB_GROUP_DIM), F32)
        h_new = jnp.zeros((MB_D_STATE, MB_GROUP_DIM), F32)
        h_decay = jnp.zeros((1, MB_GROUP_DIM), F32)
        for j in range(MB_HPG):
            hh = g * MB_HPG + j
            head = (lane2 >= j * MB_HEAD_DIM) & (lane2 < (j + 1) * MB_HEAD_DIM)
            a_l = jnp.broadcast_to(acol[:, hh:hh + 1], (L, L))
            seg = a_l - arow[hh:hh + 1, :]
            decay = jnp.exp(jnp.where(causal, seg, -jnp.inf))
            wts = cb * decay * dt_t[hh:hh + 1, :]
            cs = cg * jnp.exp(a_l)
            lhs = jnp.concatenate([wts, cs], axis=1).astype(BF16)
            xh = jnp.where(head, xgb, jnp.zeros_like(xgb))
            hb = jnp.where(head, h_old_b, jnp.zeros_like(h_old_b))
            y = y + _dot(lhs, jnp.concatenate([xh, hb], axis=0))
            h_new = h_new + _dot((bg_t * w_in_state[hh:hh + 1, :]).astype(BF16), xh)
            sd = state_decay[hh:hh + 1, :]
            h_decay = h_decay + jnp.where(head, jnp.concatenate([sd, sd], axis=1), 0.0)
        st_ref[g] = h_old * h_decay + h_new

        y = y + d_ref[:, x0:x0 + MB_GROUP_DIM] * xg
        y = y * _silu(z_ref[:, x0:x0 + MB_GROUP_DIM].astype(F32))
        y = y * lax.rsqrt(jnp.mean(y * y, axis=-1, keepdims=True) + 1e-5)
        o_ref[:, x0:x0 + MB_GROUP_DIM] = (y * ng_ref[:, x0:x0 + MB_GROUP_DIM]).astype(o_ref.dtype)


def _ssd_scan(xz, dt_raw, w_conv, b_conv, dt_bias, a_log, d_skip, norm_g, batch, seq):
    L = MB_CHUNK
    nc = seq // L
    pad = LANES - MB_HEADS
    dtb = jnp.pad(dt_bias, (0, pad)).reshape(1, LANES)
    alog = jnp.pad(a_log, (0, pad)).reshape(1, LANES)
    d_chan = jnp.repeat(d_skip, MB_HEAD_DIM).reshape(1, MB_D_INNER)
    fixed = lambda b, c: (0, 0)
    z_col = MB_XBC // MB_D_INNER
    return pl.pallas_call(
        _ssd_kernel,
        grid=(batch, nc),
        in_specs=[pl.BlockSpec((L, MB_XBC), lambda b, c: (b * nc + c, 0)),
                  pl.BlockSpec((L, MB_D_INNER), lambda b, c: (b * nc + c, z_col)),
                  pl.BlockSpec((L, LANES), lambda b, c: (b * nc + c, 0)),
                  pl.BlockSpec((MB_CONV, MB_XBC), fixed), pl.BlockSpec((1, MB_XBC), fixed),
                  pl.BlockSpec((1, LANES), fixed), pl.BlockSpec((1, LANES), fixed),
                  pl.BlockSpec((1, MB_D_INNER), fixed), pl.BlockSpec((1, MB_D_INNER), fixed)],
        out_specs=pl.BlockSpec((L, MB_D_INNER), lambda b, c: (b * nc + c, 0)),
        out_shape=jax.ShapeDtypeStruct((batch * seq, MB_D_INNER), BF16),
        scratch_shapes=[pltpu.VMEM((L + SUBLANES, MB_XBC), F32),
                        pltpu.VMEM((L, MB_XBC), F32),
                        pltpu.VMEM((MB_GROUPS, MB_D_STATE, MB_GROUP_DIM), F32)],
        compiler_params=_params(2),
        name="ssd_scan",
    )(xz, xz, dt_raw, w_conv, b_conv.reshape(1, MB_XBC), dtb, alog, d_chan,
      norm_g.reshape(1, MB_D_INNER))


def _ssd_mixer(xb, w_in, w_conv, b_conv, dt_bias, a_log, d_skip, norm_g, batch, seq):
    w_z = w_in[:, :MB_D_INNER]
    w_xbc = w_in[:, MB_D_INNER:MB_D_INNER + MB_XBC]
    w_dt = jnp.pad(w_in[:, MB_D_INNER + MB_XBC:], ((0, 0), (0, LANES - MB_HEADS)))
    xz = _mm(xb, jnp.concatenate([w_xbc, w_z], axis=1).astype(BF16), BF16, 1024, 1024)
    dt_raw = _mm(xb, w_dt.astype(BF16), F32, 1024, LANES)
    return _ssd_scan(xz, dt_raw, w_conv, b_conv, dt_bias, a_log, d_skip, norm_g, batch, seq)


def _qkv_kernel(x_ref, w_ref, pos_ref, inv_ref, m1_ref, m2_ref, o_ref):
    j = pl.program_id(1)
    y = _dot(x_ref[...], w_ref[...])

    @pl.when(j == 2)
    def _():
        o_ref[...] = y.astype(o_ref.dtype)

    @pl.when(j < 2)
    def _():
        ang = pos_ref[...] * inv_ref[...]
        cos = jnp.cos(ang)
        sin = jnp.sin(ang)
        s_up = sin * m1_ref[...]
        s_dn = sin * m2_ref[...]
        half = ROPE_DIM // 2
        for v in range(D_MODEL // LANES):
            blk = y[:, v * LANES:(v + 1) * LANES]
            rot = (blk * cos + pltpu.roll(blk, LANES - half, 1) * s_up
                   + pltpu.roll(blk, half, 1) * s_dn)
            o_ref[:, v * LANES:(v + 1) * LANES] = rot.astype(o_ref.dtype)


def _qkv_rope(xb, w_qkv, pos):
    t = xb.shape[0]
    tm = _row_tile(t, 1024)
    half = ROPE_DIM // 2
    lane = jnp.arange(LANES) % DA_HEAD_DIM
    inv = ROPE_THETA ** (-jnp.arange(0, ROPE_DIM, 2, dtype=F32) / ROPE_DIM)
    inv_lane = jnp.where(lane < ROPE_DIM, inv[lane % half], 0.0).astype(F32).reshape(1, LANES)
    m1 = jnp.where(lane < half, -1.0, 0.0).astype(F32).reshape(1, LANES)
    m2 = jnp.where((lane >= half) & (lane < ROPE_DIM), 1.0, 0.0).astype(F32).reshape(1, LANES)
    scale = jnp.concatenate([jnp.full((D_MODEL,), DA_HEAD_DIM ** -0.5, F32),
                             jnp.ones((2 * D_MODEL,), F32)])
    w = (w_qkv * scale).astype(BF16)
    fixed = lambda i, j: (0, 0)
    return pl.pallas_call(
        _qkv_kernel,
        grid=(t // tm, 3),
        in_specs=[pl.BlockSpec((tm, D_MODEL), lambda i, j: (i, 0)),
                  pl.BlockSpec((D_MODEL, D_MODEL), lambda i, j: (0, j)),
                  pl.BlockSpec((tm, 1), lambda i, j: (i, 0)),
                  pl.BlockSpec((1, LANES), fixed), pl.BlockSpec((1, LANES), fixed),
                  pl.BlockSpec((1, LANES), fixed)],
        out_specs=pl.BlockSpec((tm, D_MODEL), lambda i, j: (i, j)),
        out_shape=jax.ShapeDtypeStruct((t, 3 * D_MODEL), BF16),
        compiler_params=_params(2),
        name="qkv_rope",
    )(xb, w, pos, inv_lane, m1, m2)


def _diff_attn_kernel(q_ref, k_ref, v_ref, lq1_ref, lk1_ref, lq2_ref, lk2_ref, g_ref, o_ref,
                      *, tq, lambda_init):
    i = pl.program_id(2)
    lane = lax.broadcasted_iota(jnp.int32, (1, DA_V_DIM), 1)
    q = q_ref[...]
    zero = jnp.zeros_like(q)
    qs = (jnp.where(lane < DA_HEAD_DIM, q, zero), jnp.where(lane >= DA_HEAD_DIM, q, zero))

    def step(kt, vt, carry, mask):
        new = []
        for comp in range(2):
            m, l, acc = carry[comp]
            s = _dot_nt(qs[comp], kt)
            if mask is not None:
                s = jnp.where(mask, s, -jnp.inf)
            m_new = jnp.maximum(m, jnp.max(s, axis=-1, keepdims=True))
            alpha = jnp.exp(m - m_new)
            p = jnp.exp(s - m_new)
            l = alpha * l + jnp.sum(p, axis=-1, keepdims=True)
            acc = alpha * acc + _dot(p.astype(BF16), vt)
            new.append((m_new, l, acc))
        return tuple(new)

    def body(j, carry):
        off = pl.multiple_of(j * tq, tq)
        return step(k_ref[pl.ds(off, tq), :], v_ref[pl.ds(off, tq), :], carry, None)

    init_one = (jnp.full((tq, 1), -jnp.inf, F32), jnp.zeros((tq, 1), F32),
                jnp.zeros((tq, DA_V_DIM), F32))
    carry = lax.fori_loop(0, i, body, (init_one, init_one))
    off = pl.multiple_of(i * tq, tq)
    diag = (lax.broadcasted_iota(jnp.int32, (tq, tq), 0)
            >= lax.broadcasted_iota(jnp.int32, (tq, tq), 1))
    carry = step(k_ref[pl.ds(off, tq), :], v_ref[pl.ds(off, tq), :], carry, diag)

    lam = (jnp.exp(jnp.sum(lq1_ref[...] * lk1_ref[...], axis=-1, keepdims=True))
           - jnp.exp(jnp.sum(lq2_ref[...] * lk2_ref[...], axis=-1, keepdims=True)) + lambda_init)
    (_, l0, a0), (_, l1, a1) = carry
    o = a0 / l0 - lam * (a1 / l1)
    o = o * lax.rsqrt(jnp.mean(o * o, axis=-1, keepdims=True) + 1e-5) * g_ref[...]
    o_ref[...] = (o * (1.0 - lambda_init)).astype(o_ref.dtype)


def _diff_attn(qkv, lq1, lk1, lq2, lk2, subln_g, lambda_init, batch, seq):
    tq = _row_tile(seq, 256)
    nq = seq // tq
    vec = lambda a: jnp.pad(a, (0, LANES - a.shape[0])).reshape(1, LANES)
    fixed = lambda b, h, i: (0, 0)
    return pl.pallas_call(
        functools.partial(_diff_attn_kernel, tq=tq, lambda_init=lambda_init),
        grid=(batch, DA_HEADS, nq),
        in_specs=[pl.BlockSpec((tq, DA_V_DIM), lambda b, h, i: (b * nq + i, h)),
                  pl.BlockSpec((seq, DA_V_DIM), lambda b, h, i: (b, DA_HEADS + h)),
                  pl.BlockSpec((seq, DA_V_DIM), lambda b, h, i: (b, 2 * DA_HEADS + h)),
                  pl.BlockSpec((1, LANES), fixed), pl.BlockSpec((1, LANES), fixed),
                  pl.BlockSpec((1, LANES), fixed), pl.BlockSpec((1, LANES), fixed),
                  pl.BlockSpec((1, DA_V_DIM), fixed)],
        out_specs=pl.BlockSpec((tq, DA_V_DIM), lambda b, h, i: (b * nq + i, h)),
        out_shape=jax.ShapeDtypeStruct((batch * seq, D_MODEL), BF16),
        compiler_params=_params(3),
        name="diff_attn",
    )(qkv, qkv, qkv, vec(lq1), vec(lk1), vec(lq2), vec(lk2), subln_g.reshape(1, DA_V_DIM))


def _xa_kv_kernel(mem_ref, wkt_ref, wv_ref, kt_ref, v_ref):
    m = mem_ref[0]
    kt_ref[0] = _dot_nt(wkt_ref[...], m).astype(kt_ref.dtype)
    v_ref[0] = _dot(m, wv_ref[...]).astype(v_ref.dtype)


def _xa_kv(memb, w_kv):
    batch, n_mem, _ = memb.shape
    wkt = w_kv[:, :D_MODEL].T.astype(BF16)
    wv = w_kv[:, D_MODEL:].astype(BF16)
    return pl.pallas_call(
        _xa_kv_kernel,
        grid=(batch,),
        in_specs=[pl.BlockSpec((1, n_mem, D_MODEL), lambda b: (b, 0, 0)),
                  pl.BlockSpec((D_MODEL, D_MODEL), lambda b: (0, 0)),
                  pl.BlockSpec((D_MODEL, D_MODEL), lambda b: (0, 0))],
        out_specs=[pl.BlockSpec((1, D_MODEL, n_mem), lambda b: (b, 0, 0)),
                   pl.BlockSpec((1, n_mem, D_MODEL), lambda b: (b, 0, 0))],
        out_shape=[jax.ShapeDtypeStruct((batch, D_MODEL, n_mem), BF16),
                   jax.ShapeDtypeStruct((batch, n_mem, D_MODEL), BF16)],
        compiler_params=_params(1),
        name="xa_kv",
    )(memb, wkt, wv)


def _xattn_kernel(xb_ref, xf_ref, wq_ref, kt_ref, v_ref, wo_ref, g_ref, b_ref, of_ref, ob_ref):
    q = _dot(xb_ref[...], wq_ref[...]).astype(BF16)
    heads = []
    for h in range(XA_HEADS):
        sl = slice(h * XA_HEAD_DIM, (h + 1) * XA_HEAD_DIM)
        s = _dot(q[:, sl], kt_ref[0, sl, :])
        p = jnp.exp(s - jnp.max(s, axis=-1, keepdims=True))
        o = _dot(p.astype(BF16), v_ref[0, :, sl]) / jnp.sum(p, axis=-1, keepdims=True)
        heads.append(o.astype(BF16))
    y = _dot(jnp.concatenate(heads, axis=1), wo_ref[...])
    out = _layer_norm(DN_ALPHA * xf_ref[...] + y, g_ref[...], b_ref[...])
    of_ref[...] = out
    ob_ref[...] = out.astype(BF16)


def _xattn(xf, xb, memb, w_q, w_kv, w_out, g, b, batch, seq):
    kt, v = _xa_kv(memb, w_kv)
    n_mem = memb.shape[1]
    tm = _row_tile(seq, 512)
    nt = seq // tm
    wq = (w_q * (XA_HEAD_DIM ** -0.5)).astype(BF16)
    row = lambda bi, i: (bi * nt + i, 0)
    fixed = lambda bi, i: (0, 0)
    return pl.pallas_call(
        _xattn_kernel,
        grid=(batch, nt),
        in_specs=[pl.BlockSpec((tm, D_MODEL), row), pl.BlockSpec((tm, D_MODEL), row),
                  pl.BlockSpec((D_MODEL, D_MODEL), fixed),
                  pl.BlockSpec((1, D_MODEL, n_mem), lambda bi, i: (bi, 0, 0)),
                  pl.BlockSpec((1, n_mem, D_MODEL), lambda bi, i: (bi, 0, 0)),
                  pl.BlockSpec((D_MODEL, D_MODEL), fixed),
                  pl.BlockSpec((1, D_MODEL), fixed), pl.BlockSpec((1, D_MODEL), fixed)],
        out_specs=[pl.BlockSpec((tm, D_MODEL), row), pl.BlockSpec((tm, D_MODEL), row)],
        out_shape=[jax.ShapeDtypeStruct((batch * seq, D_MODEL), F32),
                   jax.ShapeDtypeStruct((batch * seq, D_MODEL), BF16)],
        compiler_params=_params(2),
        name="xattn",
    )(xb, xf, wq, kt, v, w_out.astype(BF16), g.reshape(1, D_MODEL), b.reshape(1, D_MODEL))


FF_CHUNK = 256


def _swiglu_kernel(xb_ref, xf_ref, wg_ref, wu_ref, wo_ref, g_ref, b_ref, of_ref, ob_ref, acc_ref):
    j = pl.program_id(1)
    xb = xb_ref[...]
    h = (_silu(_dot(xb, wg_ref[...])) * _dot(xb, wu_ref[...])).astype(BF16)
    part = _dot(h, wo_ref[...])

    @pl.when(j == 0)
    def _():
        acc_ref[...] = part

    @pl.when(j > 0)
    def _():
        acc_ref[...] += part

    @pl.when(j == pl.num_programs(1) - 1)
    def _():
        out = _layer_norm(DN_ALPHA * xf_ref[...] + acc_ref[...], g_ref[...], b_ref[...])
        of_ref[...] = out
        ob_ref[...] = out.astype(BF16)


def _swiglu(xf, xb, w_in, w_out, g, b):
    t = xf.shape[0]
    tm = _row_tile(t, 1024)
    nf = D_FF // FF_CHUNK
    row = lambda i, j: (i, 0)
    fixed = lambda i, j: (0, 0)
    w_in_b = w_in.astype(BF16)
    return pl.pallas_call(
        _swiglu_kernel,
        grid=(t // tm, nf),
        in_specs=[pl.BlockSpec((tm, D_MODEL), row), pl.BlockSpec((tm, D_MODEL), row),
                  pl.BlockSpec((D_MODEL, FF_CHUNK), lambda i, j: (0, j)),
                  pl.BlockSpec((D_MODEL, FF_CHUNK), lambda i, j: (0, nf + j)),
                  pl.BlockSpec((FF_CHUNK, D_MODEL), lambda i, j: (j, 0)),
                  pl.BlockSpec((1, D_MODEL), fixed), pl.BlockSpec((1, D_MODEL), fixed)],
        out_specs=[pl.BlockSpec((tm, D_MODEL), row), pl.BlockSpec((tm, D_MODEL), row)],
        out_shape=[jax.ShapeDtypeStruct((t, D_MODEL), F32),
                   jax.ShapeDtypeStruct((t, D_MODEL), BF16)],
        scratch_shapes=[pltpu.VMEM((tm, D_MODEL), F32)],
        compiler_params=_params(2),
        name="swiglu",
    )(xb, xf, w_in_b, w_in_b, w_out.astype(BF16), g.reshape(1, D_MODEL), b.reshape(1, D_MODEL))


def kernel(x, mem, positions, cv_w_in, cv_b_in, cv_w_dw, cv_b_dw, cv_ln_g, cv_ln_b, cv_w_out, cv_b_out, mb_w_in, mb_w_conv, mb_b_conv, mb_dt_bias, mb_a_log, mb_d, mb_norm_g, mb_w_out, da_w_qkv, da_lq1, da_lk1, da_lq2, da_lk2, da_subln_g, da_w_out, xa_w_q, xa_w_kv, xa_w_out, ff_w_in, ff_w_out, ln_g, ln_b):
    batch, seq, _ = x.shape
    t = batch * seq
    xf = x.reshape(t, D_MODEL)
    xb = xf.astype(BF16)
    memb = mem.astype(BF16)
    pos = positions.reshape(t, 1).astype(F32)
    no_bias = jnp.zeros((D_MODEL,), F32)
    for i in range(DEPTH):
        mixer, j = i % N_MIXERS, i // N_MIXERS
        if mixer == 0:
            u = _glu(xb, cv_w_in[j].astype(BF16), cv_b_in[j])
            h = _dwconv(u, cv_w_dw[j], cv_b_dw[j], cv_ln_g[j], cv_ln_b[j], batch, seq)
            w_out, bias = cv_w_out[j], cv_b_out[j]
        elif mixer == 1:
            h = _ssd_mixer(xb, mb_w_in[j], mb_w_conv[j], mb_b_conv[j], mb_dt_bias[j],
                           mb_a_log[j], mb_d[j], mb_norm_g[j], batch, seq)
            w_out, bias = mb_w_out[j], no_bias
        else:
            lambda_init = 0.8 - 0.6 * math.exp(-0.3 * i)
            qkv = _qkv_rope(xb, da_w_qkv[j], pos)
            h = _diff_attn(qkv, da_lq1[j], da_lk1[j], da_lq2[j], da_lk2[j], da_subln_g[j],
                           lambda_init, batch, seq)
            w_out, bias = da_w_out[j], no_bias
        xf, xb = _mm_postnorm(h, w_out.astype(BF16), bias, xf, ln_g[i, 0], ln_b[i, 0])
        xf, xb = _xattn(xf, xb, memb, xa_w_q[i], xa_w_kv[i], xa_w_out[i], ln_g[i, 1], ln_b[i, 1],
                        batch, seq)
        xf, xb = _swiglu(xf, xb, ff_w_in[i], ff_w_out[i], ln_g[i, 2], ln_b[i, 2])
    return xf.reshape(batch, seq, D_MODEL)
```

```python
import functools
import math

import jax
import jax.numpy as jnp
from jax import lax
from jax.experimental import pallas as pl
from jax.experimental.pallas import tpu as pltpu

F32 = jnp.float32
BF16 = jnp.bfloat16

D_MODEL = 1024
DEPTH = 4
N_MIXERS = 3
DN_ALPHA = (2.0 * DEPTH) ** 0.25
LN_EPS = 1e-5
CONV_WIDTH = 31
MB_D_INNER = 2 * D_MODEL
MB_HEAD_DIM = 64
MB_HEADS = MB_D_INNER // MB_HEAD_DIM
MB_D_STATE = 128
MB_GROUPS = 8
MB_HPG = MB_HEADS // MB_GROUPS
MB_GROUP_DIM = MB_D_INNER // MB_GROUPS
MB_CONV = 4
MB_CHUNK = 128
MB_XBC = MB_D_INNER + 2 * MB_GROUPS * MB_D_STATE
DA_HEAD_DIM = 64
DA_HEADS = D_MODEL // (2 * DA_HEAD_DIM)
DA_V_DIM = 2 * DA_HEAD_DIM
ROPE_THETA = 500000.0
ROPE_DIM = DA_HEAD_DIM // 4
XA_HEADS = 4
XA_HEAD_DIM = D_MODEL // XA_HEADS
D_FF = ((8 * D_MODEL + 3 * 256 - 1) // (3 * 256)) * 256

LANES = 128
SUBLANES = 8
VMEM_LIMIT_BYTES = 48 * 1024 * 1024


def _params(n_axes):
    return pltpu.CompilerParams(dimension_semantics=("arbitrary",) * n_axes,
                                vmem_limit_bytes=VMEM_LIMIT_BYTES)


def _row_tile(n_rows, want):
    t = min(n_rows, want)
    assert n_rows % t == 0, (n_rows, t)
    return t


def _dot(a, b):
    return jnp.dot(a, b, preferred_element_type=F32)


def _dot_nt(a, b):
    return lax.dot_general(a, b, (((1,), (1,)), ((), ())), preferred_element_type=F32)


def _layer_norm(r, g, b):
    mu = jnp.mean(r, axis=-1, keepdims=True)
    c = r - mu
    var = jnp.mean(c * c, axis=-1, keepdims=True)
    return c * lax.rsqrt(var + LN_EPS) * g + b


def _sigmoid(v):
    return 0.5 * jnp.tanh(0.5 * v) + 0.5


def _silu(v):
    h = 0.5 * v
    return h * jnp.tanh(h) + h


def _mm_kernel(x_ref, w_ref, o_ref):
    o_ref[...] = _dot(x_ref[...], w_ref[...]).astype(o_ref.dtype)


def _mm(x, w, out_dtype, tm, tn):
    t, k = x.shape
    n = w.shape[1]
    tm = _row_tile(t, tm)
    tn = _row_tile(n, tn)
    return pl.pallas_call(
        _mm_kernel,
        grid=(t // tm, n // tn),
        in_specs=[pl.BlockSpec((tm, k), lambda i, j: (i, 0)),
                  pl.BlockSpec((k, tn), lambda i, j: (0, j))],
        out_specs=pl.BlockSpec((tm, tn), lambda i, j: (i, j)),
        out_shape=jax.ShapeDtypeStruct((t, n), out_dtype),
        compiler_params=_params(2),
        name="mm",
    )(x, w)


def _mm_postnorm_kernel(h_ref, w_ref, bias_ref, x_ref, g_ref, b_ref, of_ref, ob_ref):
    y = _dot(h_ref[...], w_ref[...]) + bias_ref[...]
    out = _layer_norm(DN_ALPHA * x_ref[...] + y, g_ref[...], b_ref[...])
    of_ref[...] = out
    ob_ref[...] = out.astype(BF16)


def _mm_postnorm(h, w, bias, x, g, b):
    t, k = h.shape
    tm = _row_tile(t, 512)
    row = lambda i: (i, 0)
    fixed = lambda i: (0, 0)
    return pl.pallas_call(
        _mm_postnorm_kernel,
        grid=(t // tm,),
        in_specs=[pl.BlockSpec((tm, k), row), pl.BlockSpec((k, D_MODEL), fixed),
                  pl.BlockSpec((1, D_MODEL), fixed), pl.BlockSpec((tm, D_MODEL), row),
                  pl.BlockSpec((1, D_MODEL), fixed), pl.BlockSpec((1, D_MODEL), fixed)],
        out_specs=[pl.BlockSpec((tm, D_MODEL), row), pl.BlockSpec((tm, D_MODEL), row)],
        out_shape=[jax.ShapeDtypeStruct((t, D_MODEL), F32),
                   jax.ShapeDtypeStruct((t, D_MODEL), BF16)],
        compiler_params=_params(1),
        name="mm_postnorm",
    )(h, w, bias.reshape(1, D_MODEL), x, g.reshape(1, D_MODEL), b.reshape(1, D_MODEL))


def _glu_kernel(x_ref, w_ref, b_ref, o_ref):
    h = _dot(x_ref[...], w_ref[...]) + b_ref[...]
    o_ref[...] = (h[:, :D_MODEL] * _sigmoid(h[:, D_MODEL:])).astype(o_ref.dtype)


def _glu(xb, w, b):
    t = xb.shape[0]
    tm = _row_tile(t, 512)
    return pl.pallas_call(
        _glu_kernel,
        grid=(t // tm,),
        in_specs=[pl.BlockSpec((tm, D_MODEL), lambda i: (i, 0)),
                  pl.BlockSpec((D_MODEL, 2 * D_MODEL), lambda i: (0, 0)),
                  pl.BlockSpec((1, 2 * D_MODEL), lambda i: (0, 0))],
        out_specs=pl.BlockSpec((tm, D_MODEL), lambda i: (i, 0)),
        out_shape=jax.ShapeDtypeStruct((t, D_MODEL), BF16),
        compiler_params=_params(1),
        name="glu",
    )(xb, w, b.reshape(1, 2 * D_MODEL))


DW_HALO = 32
DW_ROWS = 64
DW_COLS = 256
DW_FIRST = DW_HALO - (CONV_WIDTH - 1)
DW_SHIFT_EXTRA = ((DW_FIRST + CONV_WIDTH - 2) // SUBLANES) * SUBLANES


def _dwconv_kernel(u_ref, w_ref, bdw_ref, g_ref, b_ref, o_ref, win_ref, sh_ref, acc_ref, *, ts):
    i = pl.program_id(1)

    @pl.when(i == 0)
    def _():
        win_ref[0:DW_HALO, :] = jnp.zeros((DW_HALO, D_MODEL), F32)

    @pl.when(i > 0)
    def _():
        win_ref[0:DW_HALO, :] = win_ref[ts:ts + DW_HALO, :]

    win_ref[DW_HALO:DW_HALO + ts, :] = u_ref[...].astype(F32)

    n_sh = ts + DW_SHIFT_EXTRA
    for r in range(1, SUBLANES):
        for c0 in range(0, D_MODEL, DW_COLS):
            sh_ref[r - 1, :, c0:c0 + DW_COLS] = win_ref[r:r + n_sh, c0:c0 + DW_COLS]

    for r0 in range(0, ts, DW_ROWS):
        for c0 in range(0, D_MODEL, DW_COLS):
            acc = jnp.zeros((DW_ROWS, DW_COLS), F32)
            for k in range(CONV_WIDTH):
                off = DW_FIRST + k
                r = off % SUBLANES
                a = r0 + off - r
                if r == 0:
                    src = win_ref[a:a + DW_ROWS, c0:c0 + DW_COLS]
                else:
                    src = sh_ref[r - 1, a:a + DW_ROWS, c0:c0 + DW_COLS]
                acc = acc + src * w_ref[k:k + 1, c0:c0 + DW_COLS]
            acc_ref[r0:r0 + DW_ROWS, c0:c0 + DW_COLS] = acc

    v = acc_ref[...] + bdw_ref[...]
    o_ref[...] = _silu(_layer_norm(v, g_ref[...], b_ref[...])).astype(o_ref.dtype)


def _dwconv(u, w_dw, b_dw, ln_g, ln_b, batch, seq):
    ts = _row_tile(seq, 256)
    nt = seq // ts
    w_pad = jnp.zeros((DW_HALO, D_MODEL), F32).at[:CONV_WIDTH].set(w_dw)
    fixed = lambda b, i: (0, 0)
    return pl.pallas_call(
        functools.partial(_dwconv_kernel, ts=ts),
        grid=(batch, nt),
        in_specs=[pl.BlockSpec((ts, D_MODEL), lambda b, i: (b * nt + i, 0)),
                  pl.BlockSpec((DW_HALO, D_MODEL), fixed),
                  pl.BlockSpec((1, D_MODEL), fixed), pl.BlockSpec((1, D_MODEL), fixed),
                  pl.BlockSpec((1, D_MODEL), fixed)],
        out_specs=pl.BlockSpec((ts, D_MODEL), lambda b, i: (b * nt + i, 0)),
        out_shape=jax.ShapeDtypeStruct((batch * seq, D_MODEL), BF16),
        scratch_shapes=[pltpu.VMEM((ts + DW_HALO, D_MODEL), F32),
                        pltpu.VMEM((SUBLANES - 1, ts + DW_SHIFT_EXTRA, D_MODEL), F32),
                        pltpu.VMEM((ts, D_MODEL), F32)],
        compiler_params=_params(2),
        name="dwconv",
    )(u, w_pad, b_dw.reshape(1, D_MODEL), ln_g.reshape(1, D_MODEL), ln_b.reshape(1, D_MODEL))


def _split3(v):
    h1 = v.astype(BF16)
    r1 = v - h1.astype(F32)
    h2 = r1.astype(BF16)
    r2 = r1 - h2.astype(F32)
    return h1, h2, r2.astype(BF16)


def _ssd_kernel(xbc_ref, z_ref, dt_ref, wc_ref, bc_ref, dtb_ref, alog_ref, d_ref, ng_ref,
                o_ref, win_ref, xc_ref, st_ref):
    L = MB_CHUNK
    c = pl.program_id(1)

    @pl.when(c == 0)
    def _():
        win_ref[0:SUBLANES, :] = jnp.zeros((SUBLANES, MB_XBC), F32)
        st_ref[...] = jnp.zeros(st_ref.shape, F32)

    @pl.when(c > 0)
    def _():
        win_ref[0:SUBLANES, :] = win_ref[L:L + SUBLANES, :]

    win_ref[SUBLANES:SUBLANES + L, :] = xbc_ref[...].astype(F32)

    cw = 512
    for c0 in range(0, MB_XBC, cw):
        acc = bc_ref[:, c0:c0 + cw] + jnp.zeros((L, cw), F32)
        for k in range(MB_CONV):
            off = SUBLANES - (MB_CONV - 1) + k
            acc = acc + win_ref[off:off + L, c0:c0 + cw] * wc_ref[k:k + 1, c0:c0 + cw]
        xc_ref[:, c0:c0 + cw] = _silu(acc)

    dtr = dt_ref[...] + dtb_ref[...]
    dt = jnp.maximum(dtr, 0.0) + jnp.log1p(jnp.exp(-jnp.abs(dtr)))
    dta = dt * (-jnp.exp(alog_ref[...]))
    dt_t = dt.T
    dta_t = dta.T
    rows = lax.broadcasted_iota(jnp.int32, (L, L), 0)
    cols = lax.broadcasted_iota(jnp.int32, (L, L), 1)
    causal = rows >= cols
    tril = jnp.where(causal, 1.0, 0.0).astype(BF16)
    triu = jnp.where(rows <= cols, 1.0, 0.0).astype(BF16)
    c1, c2, c3 = _split3(dta)
    acol = _dot(tril, c1) + _dot(tril, c2) + _dot(tril, c3)
    r1, r2, r3 = _split3(dta_t)
    arow = _dot(r1, triu) + _dot(r2, triu) + _dot(r3, triu)
    alast = arow[:, L - 1:L]
    state_decay = jnp.broadcast_to(jnp.exp(alast), (LANES, LANES))
    w_in_state = dt_t * jnp.exp(alast - arow)

    lane2 = lax.broadcasted_iota(jnp.int32, (1, MB_GROUP_DIM), 1)
    for g in range(MB_GROUPS):
        x0 = g * MB_GROUP_DIM
        b0 = MB_D_INNER + g * MB_D_STATE
        c0 = MB_D_INNER + MB_GROUPS * MB_D_STATE + g * MB_D_STATE
        bg = xc_ref[:, b0:b0 + MB_D_STATE]
        cg = xc_ref[:, c0:c0 + MB_D_STATE]
        xg = xc_ref[:, x0:x0 + MB_GROUP_DIM]
        xgb = xg.astype(BF16)
        cb = _dot_nt(cg.astype(BF16), bg.astype(BF16))
        bg_t = bg.T
        h_old = st_ref[g]
        rhs = jnp.concatenate([xgb, h_old.astype(BF16)], axis=0)
        y = jnp.zeros((L, MB_GROUP_DIM), F32)
        h_new = jnp.zeros((MB_D_STATE, MB_GROUP_DIM), F32)
        h_decay = jnp.zeros((1, MB_GROUP_DIM), F32)
        for j in range(MB_HPG):
            hh = g * MB_HPG + j
            head = (lane2 >= j * MB_HEAD_DIM) & (lane2 < (j + 1) * MB_HEAD_DIM)
            a_l = jnp.broadcast_to(acol[:, hh:hh + 1], (L, L))
            seg = a_l - arow[hh:hh + 1, :]
            decay = jnp.exp(jnp.where(causal, seg, -jnp.inf))
            wts = cb * decay * dt_t[hh:hh + 1, :]
            cs = cg * jnp.exp(a_l)
            lhs = jnp.concatenate([wts, cs], axis=1).astype(BF16)
            y = jnp.where(head, _dot(lhs, rhs), y)
            h_new = jnp.where(head, _dot((bg_t * w_in_state[hh:hh + 1, :]).astype(BF16), xgb), h_new)
            sd = state_decay[hh:hh + 1, :]
            h_decay = h_decay + jnp.where(head, jnp.concatenate([sd, sd], axis=1), 0.0)
        st_ref[g] = h_old * h_decay + h_new

        y = y + d_ref[:, x0:x0 + MB_GROUP_DIM] * xg
        y = y * _silu(z_ref[:, x0:x0 + MB_GROUP_DIM].astype(F32))
        y = y * lax.rsqrt(jnp.mean(y * y, axis=-1, keepdims=True) + 1e-5)
        o_ref[:, x0:x0 + MB_GROUP_DIM] = (y * ng_ref[:, x0:x0 + MB_GROUP_DIM]).astype(o_ref.dtype)


def _ssd_scan(xz, dt_raw, w_conv, b_conv, dt_bias, a_log, d_skip, norm_g, batch, seq):
    L = MB_CHUNK
    nc = seq // L
    pad = LANES - MB_HEADS
    dtb = jnp.pad(dt_bias, (0, pad)).reshape(1, LANES)
    alog = jnp.pad(a_log, (0, pad)).reshape(1, LANES)
    d_chan = jnp.repeat(d_skip, MB_HEAD_DIM).reshape(1, MB_D_INNER)
    fixed = lambda b, c: (0, 0)
    z_col = MB_XBC // MB_D_INNER
    return pl.pallas_call(
        _ssd_kernel,
        grid=(batch, nc),
        in_specs=[pl.BlockSpec((L, MB_XBC), lambda b, c: (b * nc + c, 0)),
                  pl.BlockSpec((L, MB_D_INNER), lambda b, c: (b * nc + c, z_col)),
                  pl.BlockSpec((L, LANES), lambda b, c: (b * nc + c, 0)),
                  pl.BlockSpec((MB_CONV, MB_XBC), fixed), pl.BlockSpec((1, MB_XBC), fixed),
                  pl.BlockSpec((1, LANES), fixed), pl.BlockSpec((1, LANES), fixed),
                  pl.BlockSpec((1, MB_D_INNER), fixed), pl.BlockSpec((1, MB_D_INNER), fixed)],
        out_specs=pl.BlockSpec((L, MB_D_INNER), lambda b, c: (b * nc + c, 0)),
        out_shape=jax.ShapeDtypeStruct((batch * seq, MB_D_INNER), BF16),
        scratch_shapes=[pltpu.VMEM((L + SUBLANES, MB_XBC), F32),
                        pltpu.VMEM((L, MB_XBC), F32),
                        pltpu.VMEM((MB_GROUPS, MB_D_STATE, MB_GROUP_DIM), F32)],
        compiler_params=_params(2),
        name="ssd_scan",
    )(xz, xz, dt_raw, w_conv, b_conv.reshape(1, MB_XBC), dtb, alog, d_chan,
      norm_g.reshape(1, MB_D_INNER))


def _ssd_mixer(xb, w_in, w_conv, b_conv, dt_bias, a_log, d_skip, norm_g, batch, seq):
    w_z = w_in[:, :MB_D_INNER]
    w_xbc = w_in[:, MB_D_INNER:MB_D_INNER + MB_XBC]
    w_dt = jnp.pad(w_in[:, MB_D_INNER + MB_XBC:], ((0, 0), (0, LANES - MB_HEADS)))
    xz = _mm(xb, jnp.concatenate([w_xbc, w_z], axis=1).astype(BF16), BF16, 1024, 1024)
    dt_raw = _mm(xb, w_dt.astype(BF16), F32, 1024, LANES)
    return _ssd_scan(xz, dt_raw, w_conv, b_conv, dt_bias, a_log, d_skip, norm_g, batch, seq)


def _qkv_kernel(x_ref, w_ref, pos_ref, inv_ref, m1_ref, m2_ref, o_ref):
    x = x_ref[...]
    ang = pos_ref[...] * inv_ref[...]
    cos = jnp.cos(ang)
    sin = jnp.sin(ang)
    s_up = sin * m1_ref[...]
    s_dn = sin * m2_ref[...]
    half = ROPE_DIM // 2
    for c0 in range(0, 2 * D_MODEL, 2 * LANES):
        y = _dot(x, w_ref[:, c0:c0 + 2 * LANES])
        for v0 in range(0, 2 * LANES, LANES):
            blk = y[:, v0:v0 + LANES]
            rot = (blk * cos + pltpu.roll(blk, LANES - half, 1) * s_up
                   + pltpu.roll(blk, half, 1) * s_dn)
            o_ref[:, c0 + v0:c0 + v0 + LANES] = rot.astype(o_ref.dtype)
    o_ref[:, 2 * D_MODEL:] = _dot(x, w_ref[:, 2 * D_MODEL:]).astype(o_ref.dtype)


def _qkv_rope(xb, w_qkv, pos):
    t = xb.shape[0]
    tm = _row_tile(t, 512)
    half = ROPE_DIM // 2
    lane = jnp.arange(LANES) % DA_HEAD_DIM
    inv = ROPE_THETA ** (-jnp.arange(0, ROPE_DIM, 2, dtype=F32) / ROPE_DIM)
    inv_lane = jnp.where(lane < ROPE_DIM, inv[lane % half], 0.0).astype(F32).reshape(1, LANES)
    m1 = jnp.where(lane < half, -1.0, 0.0).astype(F32).reshape(1, LANES)
    m2 = jnp.where((lane >= half) & (lane < ROPE_DIM), 1.0, 0.0).astype(F32).reshape(1, LANES)
    scale = jnp.concatenate([jnp.full((D_MODEL,), DA_HEAD_DIM ** -0.5, F32),
                             jnp.ones((2 * D_MODEL,), F32)])
    w = (w_qkv * scale).astype(BF16)
    fixed = lambda i: (0, 0)
    return pl.pallas_call(
        _qkv_kernel,
        grid=(t // tm,),
        in_specs=[pl.BlockSpec((tm, D_MODEL), lambda i: (i, 0)),
                  pl.BlockSpec((D_MODEL, 3 * D_MODEL), fixed, pipeline_mode=pl.Buffered(1)),
                  pl.BlockSpec((tm, 1), lambda i: (i, 0)),
                  pl.BlockSpec((1, LANES), fixed), pl.BlockSpec((1, LANES), fixed),
                  pl.BlockSpec((1, LANES), fixed)],
        out_specs=pl.BlockSpec((tm, 3 * D_MODEL), lambda i: (i, 0)),
        out_shape=jax.ShapeDtypeStruct((t, 3 * D_MODEL), BF16),
        compiler_params=_params(1),
        name="qkv_rope",
    )(xb, w, pos, inv_lane, m1, m2)


def _diff_attn_kernel(q_ref, k_ref, v_ref, lq1_ref, lk1_ref, lq2_ref, lk2_ref, g_ref, o_ref,
                      *, tq, seq, lambda_init):
    lane = lax.broadcasted_iota(jnp.int32, (1, DA_V_DIM), 1)
    lam = (jnp.exp(jnp.sum(lq1_ref[...] * lk1_ref[...], axis=-1, keepdims=True))
           - jnp.exp(jnp.sum(lq2_ref[...] * lk2_ref[...], axis=-1, keepdims=True)) + lambda_init)
    diag = (lax.broadcasted_iota(jnp.int32, (tq, tq), 0)
            >= lax.broadcasted_iota(jnp.int32, (tq, tq), 1))
    for q0 in range(0, seq, tq):
        q = q_ref[q0:q0 + tq, :]
        zero = jnp.zeros_like(q)
        comps = (jnp.where(lane < DA_HEAD_DIM, q, zero), jnp.where(lane >= DA_HEAD_DIM, q, zero))
        k_d = k_ref[q0:q0 + tq, :]
        v_d = v_ref[q0:q0 + tq, :]
        outs = []
        for qc in comps:
            s_d = jnp.where(diag, _dot_nt(qc, k_d), -jnp.inf)
            m = jnp.max(s_d, axis=-1, keepdims=True)
            if q0 > 0:
                s_p = _dot_nt(qc, k_ref[0:q0, :])
                m = jnp.maximum(m, jnp.max(s_p, axis=-1, keepdims=True))
            p_d = jnp.exp(s_d - m)
            l = jnp.sum(p_d, axis=-1, keepdims=True)
            acc = _dot(p_d.astype(BF16), v_d)
            if q0 > 0:
                p_p = jnp.exp(s_p - m)
                l = l + jnp.sum(p_p, axis=-1, keepdims=True)
                acc = acc + _dot(p_p.astype(BF16), v_ref[0:q0, :])
            outs.append(acc / l)
        o = outs[0] - lam * outs[1]
        o = o * lax.rsqrt(jnp.mean(o * o, axis=-1, keepdims=True) + 1e-5) * g_ref[...]
        o_ref[q0:q0 + tq, :] = (o * (1.0 - lambda_init)).astype(o_ref.dtype)


def _diff_attn(qkv, lq1, lk1, lq2, lk2, subln_g, lambda_init, batch, seq):
    tq = _row_tile(seq, 512)
    vec = lambda a: jnp.pad(a, (0, LANES - a.shape[0])).reshape(1, LANES)
    fixed = lambda b, h: (0, 0)
    return pl.pallas_call(
        functools.partial(_diff_attn_kernel, tq=tq, seq=seq, lambda_init=lambda_init),
        grid=(batch, DA_HEADS),
        in_specs=[pl.BlockSpec((seq, DA_V_DIM), lambda b, h: (b, h)),
                  pl.BlockSpec((seq, DA_V_DIM), lambda b, h: (b, DA_HEADS + h)),
                  pl.BlockSpec((seq, DA_V_DIM), lambda b, h: (b, 2 * DA_HEADS + h)),
                  pl.BlockSpec((1, LANES), fixed), pl.BlockSpec((1, LANES), fixed),
                  pl.BlockSpec((1, LANES), fixed), pl.BlockSpec((1, LANES), fixed),
                  pl.BlockSpec((1, DA_V_DIM), fixed)],
        out_specs=pl.BlockSpec((seq, DA_V_DIM), lambda b, h: (b, h)),
        out_shape=jax.ShapeDtypeStruct((batch * seq, D_MODEL), BF16),
        compiler_params=_params(2),
        name="diff_attn",
    )(qkv, qkv, qkv, vec(lq1), vec(lk1), vec(lq2), vec(lk2), subln_g.reshape(1, DA_V_DIM))


def _xa_kv_kernel(mem_ref, wkt_ref, wv_ref, kt_ref, v_ref):
    m = mem_ref[0]
    kt_ref[0] = _dot_nt(wkt_ref[...], m).astype(kt_ref.dtype)
    v_ref[0] = _dot(m, wv_ref[...]).astype(v_ref.dtype)


def _xa_kv(memb, w_kv):
    batch, n_mem, _ = memb.shape
    wkt = w_kv[:, :D_MODEL].T.astype(BF16)
    wv = w_kv[:, D_MODEL:].astype(BF16)
    return pl.pallas_call(
        _xa_kv_kernel,
        grid=(batch,),
        in_specs=[pl.BlockSpec((1, n_mem, D_MODEL), lambda b: (b, 0, 0)),
                  pl.BlockSpec((D_MODEL, D_MODEL), lambda b: (0, 0)),
                  pl.BlockSpec((D_MODEL, D_MODEL), lambda b: (0, 0))],
        out_specs=[pl.BlockSpec((1, D_MODEL, n_mem), lambda b: (b, 0, 0)),
                   pl.BlockSpec((1, n_mem, D_MODEL), lambda b: (b, 0, 0))],
        out_shape=[jax.ShapeDtypeStruct((batch, D_MODEL, n_mem), BF16),
                   jax.ShapeDtypeStruct((batch, n_mem, D_MODEL), BF16)],
        compiler_params=_params(1),
        name="xa_kv",
    )(memb, wkt, wv)


def _xattn_kernel(xb_ref, xf_ref, wq_ref, kt_ref, v_ref, wo_ref, g_ref, b_ref, of_ref, ob_ref):
    q = _dot(xb_ref[...], wq_ref[...]).astype(BF16)
    heads = []
    for h in range(XA_HEADS):
        sl = slice(h * XA_HEAD_DIM, (h + 1) * XA_HEAD_DIM)
        s = _dot(q[:, sl], kt_ref[0, sl, :])
        p = jnp.exp(s - jnp.max(s, axis=-1, keepdims=True))
        o = _dot(p.astype(BF16), v_ref[0, :, sl]) / jnp.sum(p, axis=-1, keepdims=True)
        heads.append(o.astype(BF16))
    y = _dot(jnp.concatenate(heads, axis=1), wo_ref[...])
    out = _layer_norm(DN_ALPHA * xf_ref[...] + y, g_ref[...], b_ref[...])
    of_ref[...] = out
    ob_ref[...] = out.astype(BF16)


def _xattn(xf, xb, memb, w_q, w_kv, w_out, g, b, batch, seq):
    kt, v = _xa_kv(memb, w_kv)
    n_mem = memb.shape[1]
    tm = _row_tile(seq, 512)
    nt = seq // tm
    wq = (w_q * (XA_HEAD_DIM ** -0.5)).astype(BF16)
    row = lambda bi, i: (bi * nt + i, 0)
    fixed = lambda bi, i: (0, 0)
    return pl.pallas_call(
        _xattn_kernel,
        grid=(batch, nt),
        in_specs=[pl.BlockSpec((tm, D_MODEL), row), pl.BlockSpec((tm, D_MODEL), row),
                  pl.BlockSpec((D_MODEL, D_MODEL), fixed),
                  pl.BlockSpec((1, D_MODEL, n_mem), lambda bi, i: (bi, 0, 0)),
                  pl.BlockSpec((1, n_mem, D_MODEL), lambda bi, i: (bi, 0, 0)),
                  pl.BlockSpec((D_MODEL, D_MODEL), fixed),
                  pl.BlockSpec((1, D_MODEL), fixed), pl.BlockSpec((1, D_MODEL), fixed)],
        out_specs=[pl.BlockSpec((tm, D_MODEL), row), pl.BlockSpec((tm, D_MODEL), row)],
        out_shape=[jax.ShapeDtypeStruct((batch * seq, D_MODEL), F32),
                   jax.ShapeDtypeStruct((batch * seq, D_MODEL), BF16)],
        compiler_params=_params(2),
        name="xattn",
    )(xb, xf, wq, kt, v, w_out.astype(BF16), g.reshape(1, D_MODEL), b.reshape(1, D_MODEL))


FF_CHUNK = 256


def _swiglu_kernel(xb_ref, xf_ref, wg_ref, wu_ref, wo_ref, g_ref, b_ref, of_ref, ob_ref, h_ref):
    xb = xb_ref[...]
    for c0 in range(0, D_FF, FF_CHUNK):
        gate = _dot(xb, wg_ref[:, c0:c0 + FF_CHUNK])
        up = _dot(xb, wu_ref[:, c0:c0 + FF_CHUNK])
        h_ref[:, c0:c0 + FF_CHUNK] = (_silu(gate) * up).astype(BF16)
    y = _dot(h_ref[...], wo_ref[...])
    out = _layer_norm(DN_ALPHA * xf_ref[...] + y, g_ref[...], b_ref[...])
    of_ref[...] = out
    ob_ref[...] = out.astype(BF16)


def _swiglu(xf, xb, w_in, w_out, g, b):
    t = xf.shape[0]
    tm = _row_tile(t, 512)
    row = lambda i: (i, 0)
    fixed = lambda i: (0, 0)
    resident = pl.Buffered(1)
    w_in_b = w_in.astype(BF16)
    return pl.pallas_call(
        _swiglu_kernel,
        grid=(t // tm,),
        in_specs=[pl.BlockSpec((tm, D_MODEL), row), pl.BlockSpec((tm, D_MODEL), row),
                  pl.BlockSpec((D_MODEL, D_FF), lambda i: (0, 0), pipeline_mode=resident),
                  pl.BlockSpec((D_MODEL, D_FF), lambda i: (0, 1), pipeline_mode=resident),
                  pl.BlockSpec((D_FF, D_MODEL), fixed, pipeline_mode=resident),
                  pl.BlockSpec((1, D_MODEL), fixed), pl.BlockSpec((1, D_MODEL), fixed)],
        out_specs=[pl.BlockSpec((tm, D_MODEL), row), pl.BlockSpec((tm, D_MODEL), row)],
        out_shape=[jax.ShapeDtypeStruct((t, D_MODEL), F32),
                   jax.ShapeDtypeStruct((t, D_MODEL), BF16)],
        scratch_shapes=[pltpu.VMEM((tm, D_FF), BF16)],
        compiler_params=_params(1),
        name="swiglu",
    )(xb, xf, w_in_b, w_in_b, w_out.astype(BF16), g.reshape(1, D_MODEL), b.reshape(1, D_MODEL))


def kernel(x, mem, positions, cv_w_in, cv_b_in, cv_w_dw, cv_b_dw, cv_ln_g, cv_ln_b, cv_w_out, cv_b_out, mb_w_in, mb_w_conv, mb_b_conv, mb_dt_bias, mb_a_log, mb_d, mb_norm_g, mb_w_out, da_w_qkv, da_lq1, da_lk1, da_lq2, da_lk2, da_subln_g, da_w_out, xa_w_q, xa_w_kv, xa_w_out, ff_w_in, ff_w_out, ln_g, ln_b):
    batch, seq, _ = x.shape
    t = batch * seq
    xf = x.reshape(t, D_MODEL)
    xb = xf.astype(BF16)
    memb = mem.astype(BF16)
    pos = positions.reshape(t, 1).astype(F32)
    no_bias = jnp.zeros((D_MODEL,), F32)
    for i in range(DEPTH):
        mixer, j = i % N_MIXERS, i // N_MIXERS
        if mixer == 0:
            u = _glu(xb, cv_w_in[j].astype(BF16), cv_b_in[j])
            h = _dwconv(u, cv_w_dw[j], cv_b_dw[j], cv_ln_g[j], cv_ln_b[j], batch, seq)
            w_out, bias = cv_w_out[j], cv_b_out[j]
        elif mixer == 1:
            h = _ssd_mixer(xb, mb_w_in[j], mb_w_conv[j], mb_b_conv[j], mb_dt_bias[j],
                           mb_a_log[j], mb_d[j], mb_norm_g[j], batch, seq)
            w_out, bias = mb_w_out[j], no_bias
        else:
            lambda_init = 0.8 - 0.6 * math.exp(-0.3 * i)
            qkv = _qkv_rope(xb, da_w_qkv[j], pos)
            h = _diff_attn(qkv, da_lq1[j], da_lk1[j], da_lq2[j], da_lk2[j], da_subln_g[j],
                           lambda_init, batch, seq)
            w_out, bias = da_w_out[j], no_bias
        xf, xb = _mm_postnorm(h, w_out.astype(BF16), bias, xf, ln_g[i, 0], ln_b[i, 0])
        xf, xb = _xattn(xf, xb, memb, xa_w_q[i], xa_w_kv[i], xa_w_out[i], ln_g[i, 1], ln_b[i, 1],
                        batch, seq)
        xf, xb = _swiglu(xf, xb, ff_w_in[i], ff_w_out[i], ln_g[i, 2], ln_b[i, 2])
    return xf.reshape(batch, seq, D_MODEL)
```

```python
import functools
import math

import jax
import jax.numpy as jnp
import numpy as np
from jax import lax
from jax.experimental import pallas as pl
from jax.experimental.pallas import tpu as pltpu

F32 = jnp.float32
BF16 = jnp.bfloat16

D_MODEL = 1024
DEPTH = 4
N_MIXERS = 3
DN_ALPHA = (2.0 * DEPTH) ** 0.25
LN_EPS = 1e-5
CONV_WIDTH = 31
MB_D_INNER = 2 * D_MODEL
MB_HEAD_DIM = 64
MB_HEADS = MB_D_INNER // MB_HEAD_DIM
MB_D_STATE = 128
MB_GROUPS = 8
MB_HPG = MB_HEADS // MB_GROUPS
MB_GROUP_DIM = MB_D_INNER // MB_GROUPS
MB_CONV = 4
MB_CHUNK = 128
MB_CARRY = 16
MB_XBC = MB_D_INNER + 2 * MB_GROUPS * MB_D_STATE
DA_HEAD_DIM = 64
DA_HEADS = D_MODEL // (2 * DA_HEAD_DIM)
DA_V_DIM = 2 * DA_HEAD_DIM
ROPE_THETA = 500000.0
ROPE_DIM = DA_HEAD_DIM // 4
XA_HEADS = 4
XA_HEAD_DIM = D_MODEL // XA_HEADS
D_FF = ((8 * D_MODEL + 3 * 256 - 1) // (3 * 256)) * 256

LANES = 128
SUBLANES = 8
VMEM_LIMIT_BYTES = 48 * 1024 * 1024


def _params(n_axes):
    return pltpu.CompilerParams(dimension_semantics=("arbitrary",) * n_axes,
                                vmem_limit_bytes=VMEM_LIMIT_BYTES)


def _row_tile(n_rows, want):
    t = min(n_rows, want)
    assert n_rows % t == 0, (n_rows, t)
    return t


def _dot(a, b):
    return jnp.dot(a, b, preferred_element_type=F32)


def _dot_nt(a, b):
    return lax.dot_general(a, b, (((1,), (1,)), ((), ())), preferred_element_type=F32)


def _layer_norm(r, g, b):
    mu = jnp.mean(r, axis=-1, keepdims=True)
    c = r - mu
    var = jnp.mean(c * c, axis=-1, keepdims=True)
    return c * lax.rsqrt(var + LN_EPS) * g + b


def _sigmoid(v):
    return 0.5 * jnp.tanh(0.5 * v) + 0.5


def _silu(v):
    h = 0.5 * v
    return h * jnp.tanh(h) + h


def _mm_kernel(x_ref, w_ref, o_ref):
    o_ref[...] = _dot(x_ref[...], w_ref[...]).astype(o_ref.dtype)


def _mm(x, w, out_dtype, tm, tn):
    t, k = x.shape
    n = w.shape[1]
    tm = _row_tile(t, tm)
    tn = _row_tile(n, tn)
    return pl.pallas_call(
        _mm_kernel,
        grid=(t // tm, n // tn),
        in_specs=[pl.BlockSpec((tm, k), lambda i, j: (i, 0)),
                  pl.BlockSpec((k, tn), lambda i, j: (0, j))],
        out_specs=pl.BlockSpec((tm, tn), lambda i, j: (i, j)),
        out_shape=jax.ShapeDtypeStruct((t, n), out_dtype),
        compiler_params=_params(2),
        name="mm",
    )(x, w)


def _glu_kernel(x_ref, w_ref, b_ref, o_ref):
    h = _dot(x_ref[...].astype(BF16), w_ref[...]) + b_ref[...]
    o_ref[...] = (h[:, :D_MODEL] * _sigmoid(h[:, D_MODEL:])).astype(o_ref.dtype)


def _glu(x, w, b):
    t = x.shape[0]
    tm = _row_tile(t, 512)
    return pl.pallas_call(
        _glu_kernel,
        grid=(t // tm,),
        in_specs=[pl.BlockSpec((tm, D_MODEL), lambda i: (i, 0)),
                  pl.BlockSpec((D_MODEL, 2 * D_MODEL), lambda i: (0, 0)),
                  pl.BlockSpec((1, 2 * D_MODEL), lambda i: (0, 0))],
        out_specs=pl.BlockSpec((tm, D_MODEL), lambda i: (i, 0)),
        out_shape=jax.ShapeDtypeStruct((t, D_MODEL), BF16),
        compiler_params=_params(1),
        name="glu",
    )(x, w, b.reshape(1, 2 * D_MODEL))


DW_HALO = 32
DW_ROWS = 64
DW_COLS = 256
DW_FIRST = DW_HALO - (CONV_WIDTH - 1)
DW_SHIFT_EXTRA = ((DW_FIRST + CONV_WIDTH - 2) // SUBLANES) * SUBLANES


def _dwconv_kernel(u_ref, w_ref, bdw_ref, g_ref, b_ref, o_ref, win_ref, sh_ref, acc_ref, *, ts):
    i = pl.program_id(1)

    @pl.when(i == 0)
    def _():
        win_ref[0:DW_HALO, :] = jnp.zeros((DW_HALO, D_MODEL), F32)

    @pl.when(i > 0)
    def _():
        win_ref[0:DW_HALO, :] = win_ref[ts:ts + DW_HALO, :]

    win_ref[DW_HALO:DW_HALO + ts, :] = u_ref[...].astype(F32)

    n_sh = ts + DW_SHIFT_EXTRA
    for r in range(1, SUBLANES):
        for c0 in range(0, D_MODEL, DW_COLS):
            sh_ref[r - 1, :, c0:c0 + DW_COLS] = win_ref[r:r + n_sh, c0:c0 + DW_COLS]

    groups = (DW_ROWS // SUBLANES, SUBLANES, DW_COLS)
    for r0 in range(0, ts, DW_ROWS):
        for c0 in range(0, D_MODEL, DW_COLS):
            acc = jnp.zeros(groups, F32)
            for k in range(CONV_WIDTH):
                off = DW_FIRST + k
                r = off % SUBLANES
                a = r0 + off - r
                if r == 0:
                    src = win_ref[a:a + DW_ROWS, c0:c0 + DW_COLS]
                else:
                    src = sh_ref[r - 1, a:a + DW_ROWS, c0:c0 + DW_COLS]
                wk = w_ref[k * SUBLANES:(k + 1) * SUBLANES, c0:c0 + DW_COLS]
                acc = acc + src.reshape(groups) * wk
            acc_ref[r0:r0 + DW_ROWS, c0:c0 + DW_COLS] = acc.reshape(DW_ROWS, DW_COLS)

    v = acc_ref[...] + bdw_ref[...]
    o_ref[...] = _silu(_layer_norm(v, g_ref[...], b_ref[...])).astype(o_ref.dtype)


def _dwconv(u, w_dw, b_dw, ln_g, ln_b, batch, seq):
    ts = _row_tile(seq, 256)
    nt = seq // ts
    w_rep = jnp.repeat(w_dw, SUBLANES, axis=0)
    fixed = lambda b, i: (0, 0)
    return pl.pallas_call(
        functools.partial(_dwconv_kernel, ts=ts),
        grid=(batch, nt),
        in_specs=[pl.BlockSpec((ts, D_MODEL), lambda b, i: (b * nt + i, 0)),
                  pl.BlockSpec((CONV_WIDTH * SUBLANES, D_MODEL), fixed),
                  pl.BlockSpec((1, D_MODEL), fixed), pl.BlockSpec((1, D_MODEL), fixed),
                  pl.BlockSpec((1, D_MODEL), fixed)],
        out_specs=pl.BlockSpec((ts, D_MODEL), lambda b, i: (b * nt + i, 0)),
        out_shape=jax.ShapeDtypeStruct((batch * seq, D_MODEL), BF16),
        scratch_shapes=[pltpu.VMEM((ts + DW_HALO, D_MODEL), F32),
                        pltpu.VMEM((SUBLANES - 1, ts + DW_SHIFT_EXTRA, D_MODEL), F32),
                        pltpu.VMEM((ts, D_MODEL), F32)],
        compiler_params=_params(2),
        name="dwconv",
    )(u, w_rep, b_dw.reshape(1, D_MODEL), ln_g.reshape(1, D_MODEL), ln_b.reshape(1, D_MODEL))


def _split3(v):
    h1 = v.astype(BF16)
    r1 = v - h1.astype(F32)
    h2 = r1.astype(BF16)
    r2 = r1 - h2.astype(F32)
    return h1, h2, r2.astype(BF16)


def _ssd_kernel(xbc_ref, z_ref, dt_ref, wc_ref, bc_ref, dtb_ref, alog_ref, d_ref, ng_ref, sel_ref,
                o_ref, win_ref, xc_ref, st_ref):
    L = MB_CHUNK
    c = pl.program_id(1)

    @pl.when(c == 0)
    def _():
        win_ref[0:MB_CARRY, :] = jnp.zeros((MB_CARRY, MB_XBC), BF16)
        st_ref[...] = jnp.zeros(st_ref.shape, F32)

    @pl.when(c > 0)
    def _():
        win_ref[0:MB_CARRY, :] = win_ref[L:L + MB_CARRY, :]

    win_ref[MB_CARRY:MB_CARRY + L, :] = xbc_ref[...]

    cw = 512
    for c0 in range(0, MB_XBC, cw):
        delayed = _dot(sel_ref[...], win_ref[:, c0:c0 + cw])
        acc = bc_ref[:, c0:c0 + cw] + xbc_ref[:, c0:c0 + cw].astype(F32) * wc_ref[MB_CONV - 1:MB_CONV, c0:c0 + cw]
        for k in range(MB_CONV - 1):
            acc = acc + delayed[k * L:(k + 1) * L, :] * wc_ref[k:k + 1, c0:c0 + cw]
        xc_ref[:, c0:c0 + cw] = _silu(acc)

    dtr = dt_ref[...] + dtb_ref[...]
    dt = jnp.maximum(dtr, 0.0) + jnp.log1p(jnp.exp(-jnp.abs(dtr)))
    dta = dt * (-jnp.exp(alog_ref[...]))
    dt_t = dt.T
    dta_t = dta.T
    rows = lax.broadcasted_iota(jnp.int32, (L, L), 0)
    cols = lax.broadcasted_iota(jnp.int32, (L, L), 1)
    causal = rows >= cols
    tril = jnp.where(causal, 1.0, 0.0).astype(BF16)
    triu = jnp.where(rows <= cols, 1.0, 0.0).astype(BF16)
    c1, c2, c3 = _split3(dta)
    acol = _dot(tril, c1) + _dot(tril, c2) + _dot(tril, c3)
    r1, r2, r3 = _split3(dta_t)
    arow = _dot(r1, triu) + _dot(r2, triu) + _dot(r3, triu)
    alast = arow[:, L - 1:L]
    state_decay = jnp.broadcast_to(jnp.exp(alast), (LANES, LANES))
    w_in_state = dt_t * jnp.exp(alast - arow)

    lane2 = lax.broadcasted_iota(jnp.int32, (1, MB_GROUP_DIM), 1)
    for g in range(MB_GROUPS):
        x0 = g * MB_GROUP_DIM
        b0 = MB_D_INNER + g * MB_D_STATE
        c0 = MB_D_INNER + MB_GROUPS * MB_D_STATE + g * MB_D_STATE
        bg = xc_ref[:, b0:b0 + MB_D_STATE]
        cg = xc_ref[:, c0:c0 + MB_D_STATE]
        xg = xc_ref[:, x0:x0 + MB_GROUP_DIM]
        xgb = xg.astype(BF16)
        cb = _dot_nt(cg.astype(BF16), bg.astype(BF16))
        bg_t = bg.T
        h_old = st_ref[g]
        rhs = jnp.concatenate([xgb, h_old.astype(BF16)], axis=0)
        y = jnp.zeros((L, MB_GROUP_DIM), F32)
        h_new = jnp.zeros((MB_D_STATE, MB_GROUP_DIM), F32)
        h_decay = jnp.zeros((1, MB_GROUP_DIM), F32)
        for j in range(MB_HPG):
            hh = g * MB_HPG + j
            head = (lane2 >= j * MB_HEAD_DIM) & (lane2 < (j + 1) * MB_HEAD_DIM)
            a_l = jnp.broadcast_to(acol[:, hh:hh + 1], (L, L))
            seg = a_l - arow[hh:hh + 1, :]
            decay = jnp.exp(jnp.where(causal, seg, -jnp.inf))
            wts = cb * decay * dt_t[hh:hh + 1, :]
            cs = cg * jnp.exp(a_l)
            lhs = jnp.concatenate([wts, cs], axis=1).astype(BF16)
            y = jnp.where(head, _dot(lhs, rhs), y)
            h_new = jnp.where(head, _dot((bg_t * w_in_state[hh:hh + 1, :]).astype(BF16), xgb), h_new)
            sd = state_decay[hh:hh + 1, :]
            h_decay = h_decay + jnp.where(head, jnp.concatenate([sd, sd], axis=1), 0.0)
        st_ref[g] = h_old * h_decay + h_new

        y = y + d_ref[:, x0:x0 + MB_GROUP_DIM] * xg
        y = y * _silu(z_ref[:, x0:x0 + MB_GROUP_DIM].astype(F32))
        y = y * lax.rsqrt(jnp.mean(y * y, axis=-1, keepdims=True) + 1e-5)
        o_ref[:, x0:x0 + MB_GROUP_DIM] = (y * ng_ref[:, x0:x0 + MB_GROUP_DIM]).astype(o_ref.dtype)


def _ssd_scan(xz, dt_raw, w_conv, b_conv, dt_bias, a_log, d_skip, norm_g, batch, seq):
    L = MB_CHUNK
    nc = seq // L
    pad = LANES - MB_HEADS
    dtb = jnp.pad(dt_bias, (0, pad)).reshape(1, LANES)
    alog = jnp.pad(a_log, (0, pad)).reshape(1, LANES)
    d_chan = jnp.repeat(d_skip, MB_HEAD_DIM).reshape(1, MB_D_INNER)
    fixed = lambda b, c: (0, 0)
    z_col = MB_XBC // MB_D_INNER
    sel = np.zeros((MB_CONV - 1, L, MB_CARRY + L), np.float32)
    for k in range(MB_CONV - 1):
        sel[k, np.arange(L), np.arange(L) + MB_CARRY - (MB_CONV - 1) + k] = 1.0
    sel = jnp.asarray(sel.reshape(-1, MB_CARRY + L), BF16)
    return pl.pallas_call(
        _ssd_kernel,
        grid=(batch, nc),
        in_specs=[pl.BlockSpec((L, MB_XBC), lambda b, c: (b * nc + c, 0)),
                  pl.BlockSpec((L, MB_D_INNER), lambda b, c: (b * nc + c, z_col)),
                  pl.BlockSpec((L, LANES), lambda b, c: (b * nc + c, 0)),
                  pl.BlockSpec((MB_CONV, MB_XBC), fixed), pl.BlockSpec((1, MB_XBC), fixed),
                  pl.BlockSpec((1, LANES), fixed), pl.BlockSpec((1, LANES), fixed),
                  pl.BlockSpec((1, MB_D_INNER), fixed), pl.BlockSpec((1, MB_D_INNER), fixed),
                  pl.BlockSpec(sel.shape, fixed)],
        out_specs=pl.BlockSpec((L, MB_D_INNER), lambda b, c: (b * nc + c, 0)),
        out_shape=jax.ShapeDtypeStruct((batch * seq, MB_D_INNER), BF16),
        scratch_shapes=[pltpu.VMEM((MB_CARRY + L, MB_XBC), BF16),
                        pltpu.VMEM((L, MB_XBC), F32),
                        pltpu.VMEM((MB_GROUPS, MB_D_STATE, MB_GROUP_DIM), F32)],
        compiler_params=_params(2),
        name="ssd_scan",
    )(xz, xz, dt_raw, w_conv, b_conv.reshape(1, MB_XBC), dtb, alog, d_chan,
      norm_g.reshape(1, MB_D_INNER), sel)


def _ssd_mixer(xb, w_in, w_conv, b_conv, dt_bias, a_log, d_skip, norm_g, batch, seq):
    w_z = w_in[:, :MB_D_INNER]
    w_xbc = w_in[:, MB_D_INNER:MB_D_INNER + MB_XBC]
    w_dt = jnp.pad(w_in[:, MB_D_INNER + MB_XBC:], ((0, 0), (0, LANES - MB_HEADS)))
    xz = _mm(xb, jnp.concatenate([w_xbc, w_z], axis=1).astype(BF16), BF16, 1024, 1024)
    dt_raw = _mm(xb, w_dt.astype(BF16), F32, 1024, LANES)
    return _ssd_scan(xz, dt_raw, w_conv, b_conv, dt_bias, a_log, d_skip, norm_g, batch, seq)


def _qkv_kernel(x_ref, w_ref, pos_ref, inv_ref, m1_ref, m2_ref, o_ref):
    x = x_ref[...]
    ang = pos_ref[...] * inv_ref[...]
    cos = jnp.cos(ang)
    sin = jnp.sin(ang)
    s_up = sin * m1_ref[...]
    s_dn = sin * m2_ref[...]
    half = ROPE_DIM // 2
    for c0 in range(0, 2 * D_MODEL, 2 * LANES):
        y = _dot(x, w_ref[:, c0:c0 + 2 * LANES])
        for v0 in range(0, 2 * LANES, LANES):
            blk = y[:, v0:v0 + LANES]
            rot = (blk * cos + pltpu.roll(blk, LANES - half, 1) * s_up
                   + pltpu.roll(blk, half, 1) * s_dn)
            o_ref[:, c0 + v0:c0 + v0 + LANES] = rot.astype(o_ref.dtype)
    o_ref[:, 2 * D_MODEL:] = _dot(x, w_ref[:, 2 * D_MODEL:]).astype(o_ref.dtype)


def _qkv_rope(xb, w_qkv, pos):
    t = xb.shape[0]
    tm = _row_tile(t, 512)
    half = ROPE_DIM // 2
    lane = jnp.arange(LANES) % DA_HEAD_DIM
    inv = ROPE_THETA ** (-jnp.arange(0, ROPE_DIM, 2, dtype=F32) / ROPE_DIM)
    inv_lane = jnp.where(lane < ROPE_DIM, inv[lane % half], 0.0).astype(F32).reshape(1, LANES)
    m1 = jnp.where(lane < half, -1.0, 0.0).astype(F32).reshape(1, LANES)
    m2 = jnp.where((lane >= half) & (lane < ROPE_DIM), 1.0, 0.0).astype(F32).reshape(1, LANES)
    scale = jnp.concatenate([jnp.full((D_MODEL,), DA_HEAD_DIM ** -0.5, F32),
                             jnp.ones((2 * D_MODEL,), F32)])
    w = (w_qkv * scale).astype(BF16)
    fixed = lambda i: (0, 0)
    return pl.pallas_call(
        _qkv_kernel,
        grid=(t // tm,),
        in_specs=[pl.BlockSpec((tm, D_MODEL), lambda i: (i, 0)),
                  pl.BlockSpec((D_MODEL, 3 * D_MODEL), fixed, pipeline_mode=pl.Buffered(1)),
                  pl.BlockSpec((tm, 1), lambda i: (i, 0)),
                  pl.BlockSpec((1, LANES), fixed), pl.BlockSpec((1, LANES), fixed),
                  pl.BlockSpec((1, LANES), fixed)],
        out_specs=pl.BlockSpec((tm, 3 * D_MODEL), lambda i: (i, 0)),
        out_shape=jax.ShapeDtypeStruct((t, 3 * D_MODEL), BF16),
        compiler_params=_params(1),
        name="qkv_rope",
    )(xb, w, pos, inv_lane, m1, m2)


def _diff_attn_kernel(q_ref, k_ref, v_ref, lq1_ref, lk1_ref, lq2_ref, lk2_ref, g_ref, o_ref,
                      *, tq, seq, lambda_init):
    lane = lax.broadcasted_iota(jnp.int32, (1, DA_V_DIM), 1)
    lam = (jnp.exp(jnp.sum(lq1_ref[...] * lk1_ref[...], axis=-1, keepdims=True))
           - jnp.exp(jnp.sum(lq2_ref[...] * lk2_ref[...], axis=-1, keepdims=True)) + lambda_init)
    diag = (lax.broadcasted_iota(jnp.int32, (tq, tq), 0)
            >= lax.broadcasted_iota(jnp.int32, (tq, tq), 1))
    for q0 in range(0, seq, tq):
        q = q_ref[q0:q0 + tq, :]
        zero = jnp.zeros_like(q)
        comps = (jnp.where(lane < DA_HEAD_DIM, q, zero), jnp.where(lane >= DA_HEAD_DIM, q, zero))
        k_d = k_ref[q0:q0 + tq, :]
        v_d = v_ref[q0:q0 + tq, :]
        outs = []
        for qc in comps:
            s_d = jnp.where(diag, _dot_nt(qc, k_d), -jnp.inf)
            m = jnp.max(s_d, axis=-1, keepdims=True)
            if q0 > 0:
                s_p = _dot_nt(qc, k_ref[0:q0, :])
                m = jnp.maximum(m, jnp.max(s_p, axis=-1, keepdims=True))
            p_d = jnp.exp(s_d - m)
            l = jnp.sum(p_d, axis=-1, keepdims=True)
            acc = _dot(p_d.astype(BF16), v_d)
            if q0 > 0:
                p_p = jnp.exp(s_p - m)
                l = l + jnp.sum(p_p, axis=-1, keepdims=True)
                acc = acc + _dot(p_p.astype(BF16), v_ref[0:q0, :])
            outs.append(acc / l)
        o = outs[0] - lam * outs[1]
        o = o * lax.rsqrt(jnp.mean(o * o, axis=-1, keepdims=True) + 1e-5) * g_ref[...]
        o_ref[q0:q0 + tq, :] = (o * (1.0 - lambda_init)).astype(o_ref.dtype)


def _diff_attn(qkv, lq1, lk1, lq2, lk2, subln_g, lambda_init, batch, seq):
    tq = _row_tile(seq, 512)
    vec = lambda a: jnp.pad(a, (0, LANES - a.shape[0])).reshape(1, LANES)
    fixed = lambda b, h: (0, 0)
    return pl.pallas_call(
        functools.partial(_diff_attn_kernel, tq=tq, seq=seq, lambda_init=lambda_init),
        grid=(batch, DA_HEADS),
        in_specs=[pl.BlockSpec((seq, DA_V_DIM), lambda b, h: (b, h)),
                  pl.BlockSpec((seq, DA_V_DIM), lambda b, h: (b, DA_HEADS + h)),
                  pl.BlockSpec((seq, DA_V_DIM), lambda b, h: (b, 2 * DA_HEADS + h)),
                  pl.BlockSpec((1, LANES), fixed), pl.BlockSpec((1, LANES), fixed),
                  pl.BlockSpec((1, LANES), fixed), pl.BlockSpec((1, LANES), fixed),
                  pl.BlockSpec((1, DA_V_DIM), fixed)],
        out_specs=pl.BlockSpec((seq, DA_V_DIM), lambda b, h: (b, h)),
        out_shape=jax.ShapeDtypeStruct((batch * seq, D_MODEL), BF16),
        compiler_params=_params(2),
        name="diff_attn",
    )(qkv, qkv, qkv, vec(lq1), vec(lk1), vec(lq2), vec(lk2), subln_g.reshape(1, DA_V_DIM))


def _xa_kv_kernel(mem_ref, wkt_ref, wv_ref, kt_ref, v_ref):
    m = mem_ref[0]
    kt_ref[0] = _dot_nt(wkt_ref[...], m).astype(kt_ref.dtype)
    v_ref[0] = _dot(m, wv_ref[...]).astype(v_ref.dtype)


def _xa_kv(memb, w_kv):
    batch, n_mem, _ = memb.shape
    wkt = w_kv[:, :D_MODEL].T.astype(BF16)
    wv = w_kv[:, D_MODEL:].astype(BF16)
    return pl.pallas_call(
        _xa_kv_kernel,
        grid=(batch,),
        in_specs=[pl.BlockSpec((1, n_mem, D_MODEL), lambda b: (b, 0, 0)),
                  pl.BlockSpec((D_MODEL, D_MODEL), lambda b: (0, 0)),
                  pl.BlockSpec((D_MODEL, D_MODEL), lambda b: (0, 0))],
        out_specs=[pl.BlockSpec((1, D_MODEL, n_mem), lambda b: (b, 0, 0)),
                   pl.BlockSpec((1, n_mem, D_MODEL), lambda b: (b, 0, 0))],
        out_shape=[jax.ShapeDtypeStruct((batch, D_MODEL, n_mem), BF16),
                   jax.ShapeDtypeStruct((batch, n_mem, D_MODEL), BF16)],
        compiler_params=_params(1),
        name="xa_kv",
    )(memb, wkt, wv)


def _xattn_kernel(h_ref, wm_ref, bm_ref, xf_ref, g0_ref, b0_ref, wq_ref, kt_ref, v_ref, wo_ref,
                  g1_ref, b1_ref, of_ref, ob_ref):
    y0 = _dot(h_ref[...], wm_ref[...]) + bm_ref[...]
    x1 = _layer_norm(DN_ALPHA * xf_ref[...] + y0, g0_ref[...], b0_ref[...])
    q = _dot(x1.astype(BF16), wq_ref[...]).astype(BF16)
    heads = []
    for h in range(XA_HEADS):
        sl = slice(h * XA_HEAD_DIM, (h + 1) * XA_HEAD_DIM)
        s = _dot(q[:, sl], kt_ref[0, sl, :])
        p = jnp.exp(s - jnp.max(s, axis=-1, keepdims=True))
        o = _dot(p.astype(BF16), v_ref[0, :, sl]) / jnp.sum(p, axis=-1, keepdims=True)
        heads.append(o.astype(BF16))
    y = _dot(jnp.concatenate(heads, axis=1), wo_ref[...])
    out = _layer_norm(DN_ALPHA * x1 + y, g1_ref[...], b1_ref[...])
    of_ref[...] = out
    ob_ref[...] = out.astype(BF16)


def _mixer_out_xattn(h, w_mix, b_mix, xf, g0, b0, memb, w_q, w_kv, w_out, g1, b1, batch, seq):
    kt, v = _xa_kv(memb, w_kv)
    n_mem = memb.shape[1]
    k_mix = h.shape[1]
    tm = _row_tile(seq, 512)
    nt = seq // tm
    wq = (w_q * (XA_HEAD_DIM ** -0.5)).astype(BF16)
    vec = lambda a: a.reshape(1, D_MODEL)
    row = lambda bi, i: (bi * nt + i, 0)
    fixed = lambda bi, i: (0, 0)
    resident = pl.Buffered(1)
    return pl.pallas_call(
        _xattn_kernel,
        grid=(batch, nt),
        in_specs=[pl.BlockSpec((tm, k_mix), row),
                  pl.BlockSpec((k_mix, D_MODEL), fixed, pipeline_mode=resident),
                  pl.BlockSpec((1, D_MODEL), fixed), pl.BlockSpec((tm, D_MODEL), row),
                  pl.BlockSpec((1, D_MODEL), fixed), pl.BlockSpec((1, D_MODEL), fixed),
                  pl.BlockSpec((D_MODEL, D_MODEL), fixed, pipeline_mode=resident),
                  pl.BlockSpec((1, D_MODEL, n_mem), lambda bi, i: (bi, 0, 0)),
                  pl.BlockSpec((1, n_mem, D_MODEL), lambda bi, i: (bi, 0, 0)),
                  pl.BlockSpec((D_MODEL, D_MODEL), fixed, pipeline_mode=resident),
                  pl.BlockSpec((1, D_MODEL), fixed), pl.BlockSpec((1, D_MODEL), fixed)],
        out_specs=[pl.BlockSpec((tm, D_MODEL), row), pl.BlockSpec((tm, D_MODEL), row)],
        out_shape=[jax.ShapeDtypeStruct((batch * seq, D_MODEL), F32),
                   jax.ShapeDtypeStruct((batch * seq, D_MODEL), BF16)],
        compiler_params=_params(2),
        name="mixer_out_xattn",
    )(h, w_mix.astype(BF16), vec(b_mix), xf, vec(g0), vec(b0), wq, kt, v, w_out.astype(BF16),
      vec(g1), vec(b1))


FF_CHUNK = 256


def _swiglu_kernel(xb_ref, xf_ref, wg_ref, wu_ref, wo_ref, g_ref, b_ref, of_ref, ob_ref, h_ref):
    xb = xb_ref[...]
    for c0 in range(0, D_FF, FF_CHUNK):
        gate = _dot(xb, wg_ref[:, c0:c0 + FF_CHUNK])
        up = _dot(xb, wu_ref[:, c0:c0 + FF_CHUNK])
        h_ref[:, c0:c0 + FF_CHUNK] = (_silu(gate) * up).astype(BF16)
    y = _dot(h_ref[...], wo_ref[...])
    out = _layer_norm(DN_ALPHA * xf_ref[...] + y, g_ref[...], b_ref[...])
    of_ref[...] = out
    ob_ref[...] = out.astype(BF16)


def _swiglu(xf, xb, w_in, w_out, g, b):
    t = xf.shape[0]
    tm = _row_tile(t, 512)
    row = lambda i: (i, 0)
    fixed = lambda i: (0, 0)
    resident = pl.Buffered(1)
    w_in_b = w_in.astype(BF16)
    return pl.pallas_call(
        _swiglu_kernel,
        grid=(t // tm,),
        in_specs=[pl.BlockSpec((tm, D_MODEL), row), pl.BlockSpec((tm, D_MODEL), row),
                  pl.BlockSpec((D_MODEL, D_FF), lambda i: (0, 0), pipeline_mode=resident),
                  pl.BlockSpec((D_MODEL, D_FF), lambda i: (0, 1), pipeline_mode=resident),
                  pl.BlockSpec((D_FF, D_MODEL), fixed, pipeline_mode=resident),
                  pl.BlockSpec((1, D_MODEL), fixed), pl.BlockSpec((1, D_MODEL), fixed)],
        out_specs=[pl.BlockSpec((tm, D_MODEL), row), pl.BlockSpec((tm, D_MODEL), row)],
        out_shape=[jax.ShapeDtypeStruct((t, D_MODEL), F32),
                   jax.ShapeDtypeStruct((t, D_MODEL), BF16)],
        scratch_shapes=[pltpu.VMEM((tm, D_FF), BF16)],
        compiler_params=_params(1),
        name="swiglu",
    )(xb, xf, w_in_b, w_in_b, w_out.astype(BF16), g.reshape(1, D_MODEL), b.reshape(1, D_MODEL))


def kernel(x, mem, positions, cv_w_in, cv_b_in, cv_w_dw, cv_b_dw, cv_ln_g, cv_ln_b, cv_w_out, cv_b_out, mb_w_in, mb_w_conv, mb_b_conv, mb_dt_bias, mb_a_log, mb_d, mb_norm_g, mb_w_out, da_w_qkv, da_lq1, da_lk1, da_lq2, da_lk2, da_subln_g, da_w_out, xa_w_q, xa_w_kv, xa_w_out, ff_w_in, ff_w_out, ln_g, ln_b):
    batch, seq, _ = x.shape
    t = batch * seq
    xf = x.reshape(t, D_MODEL)
    xb = None
    memb = mem.astype(BF16)
    pos = positions.reshape(t, 1).astype(F32)
    no_bias = jnp.zeros((D_MODEL,), F32)
    for i in range(DEPTH):
        mixer, j = i % N_MIXERS, i // N_MIXERS
        if mixer == 0:
            u = _glu(xf if xb is None else xb, cv_w_in[j].astype(BF16), cv_b_in[j])
            h = _dwconv(u, cv_w_dw[j], cv_b_dw[j], cv_ln_g[j], cv_ln_b[j], batch, seq)
            w_mix, b_mix = cv_w_out[j], cv_b_out[j]
        elif mixer == 1:
            h = _ssd_mixer(xb, mb_w_in[j], mb_w_conv[j], mb_b_conv[j], mb_dt_bias[j],
                           mb_a_log[j], mb_d[j], mb_norm_g[j], batch, seq)
            w_mix, b_mix = mb_w_out[j], no_bias
        else:
            lambda_init = 0.8 - 0.6 * math.exp(-0.3 * i)
            qkv = _qkv_rope(xb, da_w_qkv[j], pos)
            h = _diff_attn(qkv, da_lq1[j], da_lk1[j], da_lq2[j], da_lk2[j], da_subln_g[j],
                           lambda_init, batch, seq)
            w_mix, b_mix = da_w_out[j], no_bias
        xf, xb = _mixer_out_xattn(h, w_mix, b_mix, xf, ln_g[i, 0], ln_b[i, 0], memb, xa_w_q[i],
                                  xa_w_kv[i], xa_w_out[i], ln_g[i, 1], ln_b[i, 1], batch, seq)
        xf, xb = _swiglu(xf, xb, ff_w_in[i], ff_w_out[i], ln_g[i, 2], ln_b[i, 2])
    return xf.reshape(batch, seq, D_MODEL)
```

```python
import functools
import math

import jax
import jax.numpy as jnp
import numpy as np
from jax import lax
from jax.experimental import pallas as pl
from jax.experimental.pallas import tpu as pltpu

F32 = jnp.float32
BF16 = jnp.bfloat16

D_MODEL = 1024
DEPTH = 4
N_MIXERS = 3
DN_ALPHA = (2.0 * DEPTH) ** 0.25
LN_EPS = 1e-5
CONV_WIDTH = 31
MB_D_INNER = 2 * D_MODEL
MB_HEAD_DIM = 64
MB_HEADS = MB_D_INNER // MB_HEAD_DIM
MB_D_STATE = 128
MB_GROUPS = 8
MB_HPG = MB_HEADS // MB_GROUPS
MB_GROUP_DIM = MB_D_INNER // MB_GROUPS
MB_CONV = 4
MB_CHUNK = 128
MB_CARRY = 16
MB_XBC = MB_D_INNER + 2 * MB_GROUPS * MB_D_STATE
DA_HEAD_DIM = 64
DA_HEADS = D_MODEL // (2 * DA_HEAD_DIM)
DA_V_DIM = 2 * DA_HEAD_DIM
ROPE_THETA = 500000.0
ROPE_DIM = DA_HEAD_DIM // 4
LOG2_E = math.log2(math.e)
XA_HEADS = 4
XA_HEAD_DIM = D_MODEL // XA_HEADS
D_FF = ((8 * D_MODEL + 3 * 256 - 1) // (3 * 256)) * 256

LANES = 128
SUBLANES = 8
VMEM_LIMIT_BYTES = 48 * 1024 * 1024


def _params(n_axes):
    return pltpu.CompilerParams(dimension_semantics=("arbitrary",) * n_axes,
                                vmem_limit_bytes=VMEM_LIMIT_BYTES)


def _row_tile(n_rows, want):
    t = min(n_rows, want)
    assert n_rows % t == 0, (n_rows, t)
    return t


def _dot(a, b):
    return jnp.dot(a, b, preferred_element_type=F32)


def _dot_nt(a, b):
    return lax.dot_general(a, b, (((1,), (1,)), ((), ())), preferred_element_type=F32)


def _layer_norm(r, g, b):
    mu = jnp.mean(r, axis=-1, keepdims=True)
    c = r - mu
    var = jnp.mean(c * c, axis=-1, keepdims=True)
    return c * lax.rsqrt(var + LN_EPS) * g + b


def _sigmoid(v):
    return 0.5 * jnp.tanh(0.5 * v) + 0.5


def _silu(v):
    h = 0.5 * v
    return h * jnp.tanh(h) + h


def _mm_kernel(x_ref, w_ref, o_ref):
    o_ref[...] = _dot(x_ref[...], w_ref[...]).astype(o_ref.dtype)


def _mm(x, w, out_dtype, tm, tn):
    t, k = x.shape
    n = w.shape[1]
    tm = _row_tile(t, tm)
    tn = _row_tile(n, tn)
    return pl.pallas_call(
        _mm_kernel,
        grid=(t // tm, n // tn),
        in_specs=[pl.BlockSpec((tm, k), lambda i, j: (i, 0)),
                  pl.BlockSpec((k, tn), lambda i, j: (0, j))],
        out_specs=pl.BlockSpec((tm, tn), lambda i, j: (i, j)),
        out_shape=jax.ShapeDtypeStruct((t, n), out_dtype),
        compiler_params=_params(2),
        name="mm",
    )(x, w)


def _glu_kernel(x_ref, w_ref, b_ref, o_ref):
    h = _dot(x_ref[...].astype(BF16), w_ref[...]) + b_ref[...]
    o_ref[...] = (h[:, :D_MODEL] * _sigmoid(h[:, D_MODEL:])).astype(o_ref.dtype)


def _glu(x, w, b):
    t = x.shape[0]
    tm = _row_tile(t, 512)
    return pl.pallas_call(
        _glu_kernel,
        grid=(t // tm,),
        in_specs=[pl.BlockSpec((tm, D_MODEL), lambda i: (i, 0)),
                  pl.BlockSpec((D_MODEL, 2 * D_MODEL), lambda i: (0, 0)),
                  pl.BlockSpec((1, 2 * D_MODEL), lambda i: (0, 0))],
        out_specs=pl.BlockSpec((tm, D_MODEL), lambda i: (i, 0)),
        out_shape=jax.ShapeDtypeStruct((t, D_MODEL), BF16),
        compiler_params=_params(1),
        name="glu",
    )(x, w, b.reshape(1, 2 * D_MODEL))


DW_HALO = 32
DW_ROWS = 64
DW_COLS = 256
DW_FIRST = DW_HALO - (CONV_WIDTH - 1)
DW_SHIFT_EXTRA = ((DW_FIRST + CONV_WIDTH - 2) // SUBLANES) * SUBLANES


def _dwconv_kernel(u_ref, w_ref, bdw_ref, g_ref, b_ref, o_ref, win_ref, sh_ref, acc_ref, *, ts):
    i = pl.program_id(1)

    @pl.when(i == 0)
    def _():
        win_ref[0:DW_HALO, :] = jnp.zeros((DW_HALO, D_MODEL), F32)

    @pl.when(i > 0)
    def _():
        win_ref[0:DW_HALO, :] = win_ref[ts:ts + DW_HALO, :]

    win_ref[DW_HALO:DW_HALO + ts, :] = u_ref[...].astype(F32)

    n_sh = ts + DW_SHIFT_EXTRA
    for r in range(1, SUBLANES):
        for c0 in range(0, D_MODEL, DW_COLS):
            sh_ref[r - 1, :, c0:c0 + DW_COLS] = win_ref[r:r + n_sh, c0:c0 + DW_COLS]

    groups = (DW_ROWS // SUBLANES, SUBLANES, DW_COLS)
    for r0 in range(0, ts, DW_ROWS):
        for c0 in range(0, D_MODEL, DW_COLS):
            acc = jnp.zeros(groups, F32)
            for k in range(CONV_WIDTH):
                off = DW_FIRST + k
                r = off % SUBLANES
                a = r0 + off - r
                if r == 0:
                    src = win_ref[a:a + DW_ROWS, c0:c0 + DW_COLS]
                else:
                    src = sh_ref[r - 1, a:a + DW_ROWS, c0:c0 + DW_COLS]
                wk = w_ref[k * SUBLANES:(k + 1) * SUBLANES, c0:c0 + DW_COLS]
                acc = acc + src.reshape(groups) * wk
            acc_ref[r0:r0 + DW_ROWS, c0:c0 + DW_COLS] = acc.reshape(DW_ROWS, DW_COLS)

    v = acc_ref[...] + bdw_ref[...]
    o_ref[...] = _silu(_layer_norm(v, g_ref[...], b_ref[...])).astype(o_ref.dtype)


def _dwconv(u, w_dw, b_dw, ln_g, ln_b, batch, seq):
    ts = _row_tile(seq, 256)
    nt = seq // ts
    w_rep = jnp.repeat(w_dw, SUBLANES, axis=0)
    fixed = lambda b, i: (0, 0)
    return pl.pallas_call(
        functools.partial(_dwconv_kernel, ts=ts),
        grid=(batch, nt),
        in_specs=[pl.BlockSpec((ts, D_MODEL), lambda b, i: (b * nt + i, 0)),
                  pl.BlockSpec((CONV_WIDTH * SUBLANES, D_MODEL), fixed),
                  pl.BlockSpec((1, D_MODEL), fixed), pl.BlockSpec((1, D_MODEL), fixed),
                  pl.BlockSpec((1, D_MODEL), fixed)],
        out_specs=pl.BlockSpec((ts, D_MODEL), lambda b, i: (b * nt + i, 0)),
        out_shape=jax.ShapeDtypeStruct((batch * seq, D_MODEL), BF16),
        scratch_shapes=[pltpu.VMEM((ts + DW_HALO, D_MODEL), F32),
                        pltpu.VMEM((SUBLANES - 1, ts + DW_SHIFT_EXTRA, D_MODEL), F32),
                        pltpu.VMEM((ts, D_MODEL), F32)],
        compiler_params=_params(2),
        name="dwconv",
    )(u, w_rep, b_dw.reshape(1, D_MODEL), ln_g.reshape(1, D_MODEL), ln_b.reshape(1, D_MODEL))


def _split3(v):
    h1 = v.astype(BF16)
    r1 = v - h1.astype(F32)
    h2 = r1.astype(BF16)
    r2 = r1 - h2.astype(F32)
    return h1, h2, r2.astype(BF16)


def _ssd_kernel(xbc_ref, z_ref, dt_ref, wc_ref, bc_ref, dtb_ref, alog_ref, d_ref, ng_ref, sel_ref,
                o_ref, win_ref, xc_ref, st_ref):
    L = MB_CHUNK
    c = pl.program_id(1)

    @pl.when(c == 0)
    def _():
        win_ref[0:MB_CARRY, :] = jnp.zeros((MB_CARRY, MB_XBC), BF16)
        st_ref[...] = jnp.zeros(st_ref.shape, F32)

    @pl.when(c > 0)
    def _():
        win_ref[0:MB_CARRY, :] = win_ref[L:L + MB_CARRY, :]

    win_ref[MB_CARRY:MB_CARRY + L, :] = xbc_ref[...]

    cw = 512
    for c0 in range(0, MB_XBC, cw):
        delayed = _dot(sel_ref[...], win_ref[:, c0:c0 + cw])
        acc = bc_ref[:, c0:c0 + cw] + xbc_ref[:, c0:c0 + cw].astype(F32) * wc_ref[MB_CONV - 1:MB_CONV, c0:c0 + cw]
        for k in range(MB_CONV - 1):
            acc = acc + delayed[k * L:(k + 1) * L, :] * wc_ref[k:k + 1, c0:c0 + cw]
        xc_ref[:, c0:c0 + cw] = _silu(acc)

    dtr = dt_ref[...] + dtb_ref[...]
    dt = jnp.maximum(dtr, 0.0) + jnp.log1p(jnp.exp(-jnp.abs(dtr)))
    dta = dt * (-LOG2_E * jnp.exp(alog_ref[...]))
    dt_t = dt.T
    dta_t = dta.T
    rows = lax.broadcasted_iota(jnp.int32, (L, L), 0)
    cols = lax.broadcasted_iota(jnp.int32, (L, L), 1)
    causal = rows >= cols
    tril = jnp.where(causal, 1.0, 0.0).astype(BF16)
    triu = jnp.where(rows <= cols, 1.0, 0.0).astype(BF16)
    c1, c2, c3 = _split3(dta)
    acol = _dot(tril, c1) + _dot(tril, c2) + _dot(tril, c3)
    r1, r2, r3 = _split3(dta_t)
    arow = _dot(r1, triu) + _dot(r2, triu) + _dot(r3, triu)
    alast = arow[:, L - 1:L]
    state_decay = jnp.broadcast_to(jnp.exp2(alast), (LANES, LANES))
    w_in_state = dt_t * jnp.exp2(alast - arow)

    lane2 = lax.broadcasted_iota(jnp.int32, (1, MB_GROUP_DIM), 1)
    for g in range(MB_GROUPS):
        x0 = g * MB_GROUP_DIM
        b0 = MB_D_INNER + g * MB_D_STATE
        c0 = MB_D_INNER + MB_GROUPS * MB_D_STATE + g * MB_D_STATE
        bg = xc_ref[:, b0:b0 + MB_D_STATE]
        cg = xc_ref[:, c0:c0 + MB_D_STATE]
        xg = xc_ref[:, x0:x0 + MB_GROUP_DIM]
        xgb = xg.astype(BF16)
        cb = _dot_nt(cg.astype(BF16), bg.astype(BF16))
        bg_t = bg.T
        h_old = st_ref[g]
        rhs = jnp.concatenate([xgb, h_old.astype(BF16)], axis=0)
        h_decay = jnp.zeros((1, MB_GROUP_DIM), F32)
        out_lhs, state_lhs, heads = [], [], []
        for j in range(MB_HPG):
            hh = g * MB_HPG + j
            head = (lane2 >= j * MB_HEAD_DIM) & (lane2 < (j + 1) * MB_HEAD_DIM)
            a_l = jnp.broadcast_to(acol[:, hh:hh + 1], (L, L))
            seg = a_l - arow[hh:hh + 1, :]
            decay = jnp.exp2(jnp.where(causal, seg, -jnp.inf))
            wts = cb * decay * dt_t[hh:hh + 1, :]
            cs = cg * jnp.exp2(a_l)
            out_lhs.append(jnp.concatenate([wts, cs], axis=1).astype(BF16))
            state_lhs.append((bg_t * w_in_state[hh:hh + 1, :]).astype(BF16))
            sd = state_decay[hh:hh + 1, :]
            h_decay = h_decay + jnp.where(head, jnp.concatenate([sd, sd], axis=1), 0.0)
            heads.append(head)
        y_all = _dot(jnp.concatenate(out_lhs, axis=0), rhs)
        h_all = _dot(jnp.concatenate(state_lhs, axis=0), xgb)
        y = jnp.zeros((L, MB_GROUP_DIM), F32)
        h_new = jnp.zeros((MB_D_STATE, MB_GROUP_DIM), F32)
        for j, head in enumerate(heads):
            y = jnp.where(head, y_all[j * L:(j + 1) * L, :], y)
            h_new = jnp.where(head, h_all[j * MB_D_STATE:(j + 1) * MB_D_STATE, :], h_new)
        st_ref[g] = h_old * h_decay + h_new

        y = y + d_ref[:, x0:x0 + MB_GROUP_DIM] * xg
        y = y * _silu(z_ref[:, x0:x0 + MB_GROUP_DIM].astype(F32))
        y = y * lax.rsqrt(jnp.mean(y * y, axis=-1, keepdims=True) + 1e-5)
        o_ref[:, x0:x0 + MB_GROUP_DIM] = (y * ng_ref[:, x0:x0 + MB_GROUP_DIM]).astype(o_ref.dtype)


def _ssd_scan(xz, dt_raw, w_conv, b_conv, dt_bias, a_log, d_skip, norm_g, batch, seq):
    L = MB_CHUNK
    nc = seq // L
    pad = LANES - MB_HEADS
    dtb = jnp.pad(dt_bias, (0, pad)).reshape(1, LANES)
    alog = jnp.pad(a_log, (0, pad)).reshape(1, LANES)
    d_chan = jnp.repeat(d_skip, MB_HEAD_DIM).reshape(1, MB_D_INNER)
    fixed = lambda b, c: (0, 0)
    z_col = MB_XBC // MB_D_INNER
    sel = np.zeros((MB_CONV - 1, L, MB_CARRY + L), np.float32)
    for k in range(MB_CONV - 1):
        sel[k, np.arange(L), np.arange(L) + MB_CARRY - (MB_CONV - 1) + k] = 1.0
    sel = jnp.asarray(sel.reshape(-1, MB_CARRY + L), BF16)
    return pl.pallas_call(
        _ssd_kernel,
        grid=(batch, nc),
        in_specs=[pl.BlockSpec((L, MB_XBC), lambda b, c: (b * nc + c, 0)),
                  pl.BlockSpec((L, MB_D_INNER), lambda b, c: (b * nc + c, z_col)),
                  pl.BlockSpec((L, LANES), lambda b, c: (b * nc + c, 0)),
                  pl.BlockSpec((MB_CONV, MB_XBC), fixed), pl.BlockSpec((1, MB_XBC), fixed),
                  pl.BlockSpec((1, LANES), fixed), pl.BlockSpec((1, LANES), fixed),
                  pl.BlockSpec((1, MB_D_INNER), fixed), pl.BlockSpec((1, MB_D_INNER), fixed),
                  pl.BlockSpec(sel.shape, fixed)],
        out_specs=pl.BlockSpec((L, MB_D_INNER), lambda b, c: (b * nc + c, 0)),
        out_shape=jax.ShapeDtypeStruct((batch * seq, MB_D_INNER), BF16),
        scratch_shapes=[pltpu.VMEM((MB_CARRY + L, MB_XBC), BF16),
                        pltpu.VMEM((L, MB_XBC), F32),
                        pltpu.VMEM((MB_GROUPS, MB_D_STATE, MB_GROUP_DIM), F32)],
        compiler_params=_params(2),
        name="ssd_scan",
    )(xz, xz, dt_raw, w_conv, b_conv.reshape(1, MB_XBC), dtb, alog, d_chan,
      norm_g.reshape(1, MB_D_INNER), sel)


def _ssd_mixer(xb, w_in, w_conv, b_conv, dt_bias, a_log, d_skip, norm_g, batch, seq):
    w_z = w_in[:, :MB_D_INNER]
    w_xbc = w_in[:, MB_D_INNER:MB_D_INNER + MB_XBC]
    w_dt = jnp.pad(w_in[:, MB_D_INNER + MB_XBC:], ((0, 0), (0, LANES - MB_HEADS)))
    xz = _mm(xb, jnp.concatenate([w_xbc, w_z], axis=1).astype(BF16), BF16, 1024, 1024)
    dt_raw = _mm(xb, w_dt.astype(BF16), F32, 1024, LANES)
    return _ssd_scan(xz, dt_raw, w_conv, b_conv, dt_bias, a_log, d_skip, norm_g, batch, seq)


def _qkv_kernel(x_ref, w_ref, pos_ref, inv_ref, m1_ref, m2_ref, o_ref):
    x = x_ref[...]
    ang = pos_ref[...] * inv_ref[...]
    cos = jnp.cos(ang)
    sin = jnp.sin(ang)
    s_up = sin * m1_ref[...]
    s_dn = sin * m2_ref[...]
    half = ROPE_DIM // 2
    for c0 in range(0, 2 * D_MODEL, 2 * LANES):
        y = _dot(x, w_ref[:, c0:c0 + 2 * LANES])
        for v0 in range(0, 2 * LANES, LANES):
            blk = y[:, v0:v0 + LANES]
            rot = (blk * cos + pltpu.roll(blk, LANES - half, 1) * s_up
                   + pltpu.roll(blk, half, 1) * s_dn)
            if c0 < D_MODEL:
                rot = rot * LOG2_E
            o_ref[:, c0 + v0:c0 + v0 + LANES] = rot.astype(o_ref.dtype)
    o_ref[:, 2 * D_MODEL:] = _dot(x, w_ref[:, 2 * D_MODEL:]).astype(o_ref.dtype)


def _qkv_rope(xb, w_qkv, pos):
    t = xb.shape[0]
    tm = _row_tile(t, 512)
    half = ROPE_DIM // 2
    lane = jnp.arange(LANES) % DA_HEAD_DIM
    inv = ROPE_THETA ** (-jnp.arange(0, ROPE_DIM, 2, dtype=F32) / ROPE_DIM)
    inv_lane = jnp.where(lane < ROPE_DIM, inv[lane % half], 0.0).astype(F32).reshape(1, LANES)
    m1 = jnp.where(lane < half, -1.0, 0.0).astype(F32).reshape(1, LANES)
    m2 = jnp.where((lane >= half) & (lane < ROPE_DIM), 1.0, 0.0).astype(F32).reshape(1, LANES)
    scale = jnp.concatenate([jnp.full((D_MODEL,), DA_HEAD_DIM ** -0.5, F32),
                             jnp.ones((2 * D_MODEL,), F32)])
    w = (w_qkv * scale).astype(BF16)
    fixed = lambda i: (0, 0)
    return pl.pallas_call(
        _qkv_kernel,
        grid=(t // tm,),
        in_specs=[pl.BlockSpec((tm, D_MODEL), lambda i: (i, 0)),
                  pl.BlockSpec((D_MODEL, 3 * D_MODEL), fixed, pipeline_mode=pl.Buffered(1)),
                  pl.BlockSpec((tm, 1), lambda i: (i, 0)),
                  pl.BlockSpec((1, LANES), fixed), pl.BlockSpec((1, LANES), fixed),
                  pl.BlockSpec((1, LANES), fixed)],
        out_specs=pl.BlockSpec((tm, 3 * D_MODEL), lambda i: (i, 0)),
        out_shape=jax.ShapeDtypeStruct((t, 3 * D_MODEL), BF16),
        compiler_params=_params(1),
        name="qkv_rope",
    )(xb, w, pos, inv_lane, m1, m2)


def _diff_attn_kernel(q_ref, k_ref, v_ref, lq1_ref, lk1_ref, lq2_ref, lk2_ref, g_ref, o_ref,
                      vt_ref, *, tq, seq, lambda_init):
    lane = lax.broadcasted_iota(jnp.int32, (1, DA_V_DIM), 1)
    lam = (jnp.exp(jnp.sum(lq1_ref[...] * lk1_ref[...], axis=-1, keepdims=True))
           - jnp.exp(jnp.sum(lq2_ref[...] * lk2_ref[...], axis=-1, keepdims=True)) + lambda_init)
    visible = (lax.broadcasted_iota(jnp.int32, (tq, tq), 0)
               <= lax.broadcasted_iota(jnp.int32, (tq, tq), 1))
    vt_ref[...] = v_ref[...].astype(F32).T.astype(BF16)
    for q0 in range(0, seq, tq):
        q = q_ref[q0:q0 + tq, :]
        zero = jnp.zeros_like(q)
        comps = (jnp.where(lane < DA_HEAD_DIM, q, zero), jnp.where(lane >= DA_HEAD_DIM, q, zero))
        outs = []
        for qc in comps:
            s_d = jnp.where(visible, _dot_nt(k_ref[q0:q0 + tq, :], qc), -jnp.inf)
            m = jnp.max(s_d, axis=0, keepdims=True)
            if q0 > 0:
                s_p = _dot_nt(k_ref[0:q0, :], qc)
                m = jnp.maximum(m, jnp.max(s_p, axis=0, keepdims=True))
            p_d = jnp.exp2(s_d - m)
            l = jnp.sum(p_d, axis=0, keepdims=True)
            acc = _dot(vt_ref[:, q0:q0 + tq], p_d.astype(BF16))
            if q0 > 0:
                p_p = jnp.exp2(s_p - m)
                l = l + jnp.sum(p_p, axis=0, keepdims=True)
                acc = acc + _dot(vt_ref[:, 0:q0], p_p.astype(BF16))
            outs.append(acc / l)
        o = (outs[0] - lam * outs[1]).T
        o = o * lax.rsqrt(jnp.mean(o * o, axis=-1, keepdims=True) + 1e-5) * g_ref[...]
        o_ref[q0:q0 + tq, :] = (o * (1.0 - lambda_init)).astype(o_ref.dtype)


def _diff_attn(qkv, lq1, lk1, lq2, lk2, subln_g, lambda_init, batch, seq):
    tq = _row_tile(seq, 512)
    vec = lambda a: jnp.pad(a, (0, LANES - a.shape[0])).reshape(1, LANES)
    fixed = lambda b, h: (0, 0)
    return pl.pallas_call(
        functools.partial(_diff_attn_kernel, tq=tq, seq=seq, lambda_init=lambda_init),
        grid=(batch, DA_HEADS),
        in_specs=[pl.BlockSpec((seq, DA_V_DIM), lambda b, h: (b, h)),
                  pl.BlockSpec((seq, DA_V_DIM), lambda b, h: (b, DA_HEADS + h)),
                  pl.BlockSpec((seq, DA_V_DIM), lambda b, h: (b, 2 * DA_HEADS + h)),
                  pl.BlockSpec((1, LANES), fixed), pl.BlockSpec((1, LANES), fixed),
                  pl.BlockSpec((1, LANES), fixed), pl.BlockSpec((1, LANES), fixed),
                  pl.BlockSpec((1, DA_V_DIM), fixed)],
        out_specs=pl.BlockSpec((seq, DA_V_DIM), lambda b, h: (b, h)),
        out_shape=jax.ShapeDtypeStruct((batch * seq, D_MODEL), BF16),
        scratch_shapes=[pltpu.VMEM((DA_V_DIM, seq), BF16)],
        compiler_params=_params(2),
        name="diff_attn",
    )(qkv, qkv, qkv, vec(lq1), vec(lk1), vec(lq2), vec(lk2), subln_g.reshape(1, DA_V_DIM))


def _xa_kv_kernel(mem_ref, wkt_ref, wv_ref, kt_ref, v_ref):
    m = mem_ref[0]
    kt_ref[0] = _dot_nt(wkt_ref[...], m).astype(kt_ref.dtype)
    v_ref[0] = _dot(m, wv_ref[...]).astype(v_ref.dtype)


def _xa_kv(memb, w_kv):
    batch, n_mem, _ = memb.shape
    wkt = w_kv[:, :D_MODEL].T.astype(BF16)
    wv = w_kv[:, D_MODEL:].astype(BF16)
    return pl.pallas_call(
        _xa_kv_kernel,
        grid=(batch,),
        in_specs=[pl.BlockSpec((1, n_mem, D_MODEL), lambda b: (b, 0, 0)),
                  pl.BlockSpec((D_MODEL, D_MODEL), lambda b: (0, 0)),
                  pl.BlockSpec((D_MODEL, D_MODEL), lambda b: (0, 0))],
        out_specs=[pl.BlockSpec((1, D_MODEL, n_mem), lambda b: (b, 0, 0)),
                   pl.BlockSpec((1, n_mem, D_MODEL), lambda b: (b, 0, 0))],
        out_shape=[jax.ShapeDtypeStruct((batch, D_MODEL, n_mem), BF16),
                   jax.ShapeDtypeStruct((batch, n_mem, D_MODEL), BF16)],
        compiler_params=_params(1),
        name="xa_kv",
    )(memb, wkt, wv)


def _xattn_kernel(h_ref, wm_ref, bm_ref, xf_ref, g0_ref, b0_ref, wq_ref, kt_ref, v_ref, wo_ref,
                  g1_ref, b1_ref, of_ref, ob_ref):
    y0 = _dot(h_ref[...], wm_ref[...]) + bm_ref[...]
    x1 = _layer_norm(DN_ALPHA * xf_ref[...] + y0, g0_ref[...], b0_ref[...])
    q = _dot(x1.astype(BF16), wq_ref[...]).astype(BF16)
    heads = []
    for h in range(XA_HEADS):
        sl = slice(h * XA_HEAD_DIM, (h + 1) * XA_HEAD_DIM)
        s = _dot(q[:, sl], kt_ref[0, sl, :])
        p = jnp.exp(s - jnp.max(s, axis=-1, keepdims=True))
        o = _dot(p.astype(BF16), v_ref[0, :, sl]) / jnp.sum(p, axis=-1, keepdims=True)
        heads.append(o.astype(BF16))
    o_all = jnp.concatenate(heads, axis=1)
    half = o_all.shape[0] // 2
    for r0 in (0, half):
        rows = slice(r0, r0 + half)
        y = _dot(o_all[rows, :], wo_ref[...])
        out = _layer_norm(DN_ALPHA * x1[rows, :] + y, g1_ref[...], b1_ref[...])
        of_ref[rows, :] = out
        ob_ref[rows, :] = out.astype(BF16)


def _mixer_out_xattn(h, w_mix, b_mix, xf, g0, b0, memb, w_q, w_kv, w_out, g1, b1, batch, seq):
    kt, v = _xa_kv(memb, w_kv)
    n_mem = memb.shape[1]
    k_mix = h.shape[1]
    tm = _row_tile(seq, 512)
    nt = seq // tm
    wq = (w_q * (XA_HEAD_DIM ** -0.5)).astype(BF16)
    vec = lambda a: a.reshape(1, D_MODEL)
    row = lambda bi, i: (bi * nt + i, 0)
    fixed = lambda bi, i: (0, 0)
    resident = pl.Buffered(1)
    return pl.pallas_call(
        _xattn_kernel,
        grid=(batch, nt),
        in_specs=[pl.BlockSpec((tm, k_mix), row),
                  pl.BlockSpec((k_mix, D_MODEL), fixed, pipeline_mode=resident),
                  pl.BlockSpec((1, D_MODEL), fixed), pl.BlockSpec((tm, D_MODEL), row),
                  pl.BlockSpec((1, D_MODEL), fixed), pl.BlockSpec((1, D_MODEL), fixed),
                  pl.BlockSpec((D_MODEL, D_MODEL), fixed, pipeline_mode=resident),
                  pl.BlockSpec((1, D_MODEL, n_mem), lambda bi, i: (bi, 0, 0)),
                  pl.BlockSpec((1, n_mem, D_MODEL), lambda bi, i: (bi, 0, 0)),
                  pl.BlockSpec((D_MODEL, D_MODEL), fixed, pipeline_mode=resident),
                  pl.BlockSpec((1, D_MODEL), fixed), pl.BlockSpec((1, D_MODEL), fixed)],
        out_specs=[pl.BlockSpec((tm, D_MODEL), row), pl.BlockSpec((tm, D_MODEL), row)],
        out_shape=[jax.ShapeDtypeStruct((batch * seq, D_MODEL), F32),
                   jax.ShapeDtypeStruct((batch * seq, D_MODEL), BF16)],
        compiler_params=_params(2),
        name="mixer_out_xattn",
    )(h, w_mix.astype(BF16), vec(b_mix), xf, vec(g0), vec(b0), wq, kt, v, w_out.astype(BF16),
      vec(g1), vec(b1))


FF_CHUNK = 256
FF_ROW_BLOCKS = 2


def _swiglu_kernel(xb_ref, xf_ref, wg_ref, wu_ref, wo_ref, g_ref, b_ref, of_ref, ob_ref, h_ref):
    xb = xb_ref[...]
    for c0 in range(0, D_FF, FF_CHUNK):
        gate = _dot(xb, wg_ref[:, c0:c0 + FF_CHUNK])
        up = _dot(xb, wu_ref[:, c0:c0 + FF_CHUNK])
        h_ref[:, c0:c0 + FF_CHUNK] = (_silu(gate) * up).astype(BF16)
    block = h_ref.shape[0] // FF_ROW_BLOCKS
    for r0 in range(0, h_ref.shape[0], block):
        rows = slice(r0, r0 + block)
        y = _dot(h_ref[rows, :], wo_ref[...])
        out = _layer_norm(DN_ALPHA * xf_ref[rows, :] + y, g_ref[...], b_ref[...])
        of_ref[rows, :] = out
        ob_ref[rows, :] = out.astype(BF16)


def _swiglu(xf, xb, w_in, w_out, g, b):
    t = xf.shape[0]
    tm = _row_tile(t, 512)
    row = lambda i: (i, 0)
    fixed = lambda i: (0, 0)
    resident = pl.Buffered(1)
    w_in_b = w_in.astype(BF16)
    return pl.pallas_call(
        _swiglu_kernel,
        grid=(t // tm,),
        in_specs=[pl.BlockSpec((tm, D_MODEL), row), pl.BlockSpec((tm, D_MODEL), row),
                  pl.BlockSpec((D_MODEL, D_FF), lambda i: (0, 0), pipeline_mode=resident),
                  pl.BlockSpec((D_MODEL, D_FF), lambda i: (0, 1), pipeline_mode=resident),
                  pl.BlockSpec((D_FF, D_MODEL), fixed, pipeline_mode=resident),
                  pl.BlockSpec((1, D_MODEL), fixed), pl.BlockSpec((1, D_MODEL), fixed)],
        out_specs=[pl.BlockSpec((tm, D_MODEL), row), pl.BlockSpec((tm, D_MODEL), row)],
        out_shape=[jax.ShapeDtypeStruct((t, D_MODEL), F32),
                   jax.ShapeDtypeStruct((t, D_MODEL), BF16)],
        scratch_shapes=[pltpu.VMEM((tm, D_FF), BF16)],
        compiler_params=_params(1),
        name="swiglu",
    )(xb, xf, w_in_b, w_in_b, w_out.astype(BF16), g.reshape(1, D_MODEL), b.reshape(1, D_MODEL))


def kernel(x, mem, positions, cv_w_in, cv_b_in, cv_w_dw, cv_b_dw, cv_ln_g, cv_ln_b, cv_w_out, cv_b_out, mb_w_in, mb_w_conv, mb_b_conv, mb_dt_bias, mb_a_log, mb_d, mb_norm_g, mb_w_out, da_w_qkv, da_lq1, da_lk1, da_lq2, da_lk2, da_subln_g, da_w_out, xa_w_q, xa_w_kv, xa_w_out, ff_w_in, ff_w_out, ln_g, ln_b):
    batch, seq, _ = x.shape
    t = batch * seq
    xf = x.reshape(t, D_MODEL)
    xb = None
    memb = mem.astype(BF16)
    pos = positions.reshape(t, 1).astype(F32)
    no_bias = jnp.zeros((D_MODEL,), F32)
    for i in range(DEPTH):
        mixer, j = i % N_MIXERS, i // N_MIXERS
        if mixer == 0:
            u = _glu(xf if xb is None else xb, cv_w_in[j].astype(BF16), cv_b_in[j])
            h = _dwconv(u, cv_w_dw[j], cv_b_dw[j], cv_ln_g[j], cv_ln_b[j], batch, seq)
            w_mix, b_mix = cv_w_out[j], cv_b_out[j]
        elif mixer == 1:
            h = _ssd_mixer(xb, mb_w_in[j], mb_w_conv[j], mb_b_conv[j], mb_dt_bias[j],
                           mb_a_log[j], mb_d[j], mb_norm_g[j], batch, seq)
            w_mix, b_mix = mb_w_out[j], no_bias
        else:
            lambda_init = 0.8 - 0.6 * math.exp(-0.3 * i)
            qkv = _qkv_rope(xb, da_w_qkv[j], pos)
            h = _diff_attn(qkv, da_lq1[j], da_lk1[j], da_lq2[j], da_lk2[j], da_subln_g[j],
                           lambda_init, batch, seq)
            w_mix, b_mix = da_w_out[j], no_bias
        xf, xb = _mixer_out_xattn(h, w_mix, b_mix, xf, ln_g[i, 0], ln_b[i, 0], memb, xa_w_q[i],
                                  xa_w_kv[i], xa_w_out[i], ln_g[i, 1], ln_b[i, 1], batch, seq)
        xf, xb = _swiglu(xf, xb, ff_w_in[i], ff_w_out[i], ln_g[i, 2], ln_b[i, 2])
    return xf.reshape(batch, seq, D_MODEL)
```

```python
import functools
import math

import jax
import jax.numpy as jnp
import numpy as np
from jax import lax
from jax.experimental import pallas as pl
from jax.experimental.pallas import tpu as pltpu

F32 = jnp.float32
BF16 = jnp.bfloat16

D_MODEL = 1024
DEPTH = 4
N_MIXERS = 3
DN_ALPHA = (2.0 * DEPTH) ** 0.25
LN_EPS = 1e-5
CONV_WIDTH = 31
MB_D_INNER = 2 * D_MODEL
MB_HEAD_DIM = 64
MB_HEADS = MB_D_INNER // MB_HEAD_DIM
MB_D_STATE = 128
MB_GROUPS = 8
MB_HPG = MB_HEADS // MB_GROUPS
MB_GROUP_DIM = MB_D_INNER // MB_GROUPS
MB_CONV = 4
MB_CHUNK = 128
MB_CARRY = 16
MB_CHUNKS_PER_STEP = 4
MB_XBC = MB_D_INNER + 2 * MB_GROUPS * MB_D_STATE
DA_HEAD_DIM = 64
DA_HEADS = D_MODEL // (2 * DA_HEAD_DIM)
DA_V_DIM = 2 * DA_HEAD_DIM
ROPE_THETA = 500000.0
ROPE_DIM = DA_HEAD_DIM // 4
LOG2_E = math.log2(math.e)
XA_HEADS = 4
XA_HEAD_DIM = D_MODEL // XA_HEADS
D_FF = ((8 * D_MODEL + 3 * 256 - 1) // (3 * 256)) * 256

LANES = 128
SUBLANES = 8
VMEM_LIMIT_BYTES = 48 * 1024 * 1024


def _params(n_axes):
    return pltpu.CompilerParams(dimension_semantics=("arbitrary",) * n_axes,
                                vmem_limit_bytes=VMEM_LIMIT_BYTES)


def _row_tile(n_rows, want):
    t = min(n_rows, want)
    assert n_rows % t == 0, (n_rows, t)
    return t


def _dot(a, b):
    return jnp.dot(a, b, preferred_element_type=F32)


def _dot_nt(a, b):
    return lax.dot_general(a, b, (((1,), (1,)), ((), ())), preferred_element_type=F32)


def _layer_norm(r, g, b):
    mu = jnp.mean(r, axis=-1, keepdims=True)
    c = r - mu
    var = jnp.mean(c * c, axis=-1, keepdims=True)
    return c * lax.rsqrt(var + LN_EPS) * g + b


def _sigmoid(v):
    return 0.5 * jnp.tanh(0.5 * v) + 0.5


def _silu(v):
    h = 0.5 * v
    return h * jnp.tanh(h) + h


def _mm_kernel(x_ref, w_ref, o_ref):
    o_ref[...] = _dot(x_ref[...], w_ref[...]).astype(o_ref.dtype)


def _mm(x, w, out_dtype, tm, tn):
    t, k = x.shape
    n = w.shape[1]
    tm = _row_tile(t, tm)
    tn = _row_tile(n, tn)
    return pl.pallas_call(
        _mm_kernel,
        grid=(t // tm, n // tn),
        in_specs=[pl.BlockSpec((tm, k), lambda i, j: (i, 0)),
                  pl.BlockSpec((k, tn), lambda i, j: (0, j))],
        out_specs=pl.BlockSpec((tm, tn), lambda i, j: (i, j)),
        out_shape=jax.ShapeDtypeStruct((t, n), out_dtype),
        compiler_params=_params(2),
        name="mm",
    )(x, w)


def _glu_kernel(x_ref, w_ref, b_ref, o_ref):
    h = _dot(x_ref[...].astype(BF16), w_ref[...]) + b_ref[...]
    o_ref[...] = (h[:, :D_MODEL] * _sigmoid(h[:, D_MODEL:])).astype(o_ref.dtype)


def _glu(x, w, b):
    t = x.shape[0]
    tm = _row_tile(t, 1024)
    return pl.pallas_call(
        _glu_kernel,
        grid=(t // tm,),
        in_specs=[pl.BlockSpec((tm, D_MODEL), lambda i: (i, 0)),
                  pl.BlockSpec((D_MODEL, 2 * D_MODEL), lambda i: (0, 0)),
                  pl.BlockSpec((1, 2 * D_MODEL), lambda i: (0, 0))],
        out_specs=pl.BlockSpec((tm, D_MODEL), lambda i: (i, 0)),
        out_shape=jax.ShapeDtypeStruct((t, D_MODEL), BF16),
        compiler_params=_params(1),
        name="glu",
    )(x, w, b.reshape(1, 2 * D_MODEL))


DW_HALO = 32
DW_ROWS = 64
DW_COLS = 256
DW_FIRST = DW_HALO - (CONV_WIDTH - 1)
DW_SHIFT_EXTRA = ((DW_FIRST + CONV_WIDTH - 2) // SUBLANES) * SUBLANES


def _dwconv_kernel(u_ref, w_ref, bdw_ref, g_ref, b_ref, o_ref, win_ref, sh_ref, acc_ref, *, ts):
    i = pl.program_id(1)

    @pl.when(i == 0)
    def _():
        win_ref[0:DW_HALO, :] = jnp.zeros((DW_HALO, D_MODEL), F32)

    @pl.when(i > 0)
    def _():
        win_ref[0:DW_HALO, :] = win_ref[ts:ts + DW_HALO, :]

    win_ref[DW_HALO:DW_HALO + ts, :] = u_ref[...].astype(F32)

    n_sh = ts + DW_SHIFT_EXTRA
    for r in range(1, SUBLANES):
        for c0 in range(0, D_MODEL, DW_COLS):
            sh_ref[r - 1, :, c0:c0 + DW_COLS] = win_ref[r:r + n_sh, c0:c0 + DW_COLS]

    groups = (DW_ROWS // SUBLANES, SUBLANES, DW_COLS)
    for r0 in range(0, ts, DW_ROWS):
        for c0 in range(0, D_MODEL, DW_COLS):
            acc = jnp.zeros(groups, F32)
            for k in range(CONV_WIDTH):
                off = DW_FIRST + k
                r = off % SUBLANES
                a = r0 + off - r
                if r == 0:
                    src = win_ref[a:a + DW_ROWS, c0:c0 + DW_COLS]
                else:
                    src = sh_ref[r - 1, a:a + DW_ROWS, c0:c0 + DW_COLS]
                wk = w_ref[k * SUBLANES:(k + 1) * SUBLANES, c0:c0 + DW_COLS]
                acc = acc + src.reshape(groups) * wk
            acc_ref[r0:r0 + DW_ROWS, c0:c0 + DW_COLS] = acc.reshape(DW_ROWS, DW_COLS)

    v = acc_ref[...] + bdw_ref[...]
    o_ref[...] = _silu(_layer_norm(v, g_ref[...], b_ref[...])).astype(o_ref.dtype)


def _dwconv(u, w_dw, b_dw, ln_g, ln_b, batch, seq):
    ts = _row_tile(seq, 256)
    nt = seq // ts
    w_rep = jnp.repeat(w_dw, SUBLANES, axis=0)
    fixed = lambda b, i: (0, 0)
    return pl.pallas_call(
        functools.partial(_dwconv_kernel, ts=ts),
        grid=(batch, nt),
        in_specs=[pl.BlockSpec((ts, D_MODEL), lambda b, i: (b * nt + i, 0)),
                  pl.BlockSpec((CONV_WIDTH * SUBLANES, D_MODEL), fixed),
                  pl.BlockSpec((1, D_MODEL), fixed), pl.BlockSpec((1, D_MODEL), fixed),
                  pl.BlockSpec((1, D_MODEL), fixed)],
        out_specs=pl.BlockSpec((ts, D_MODEL), lambda b, i: (b * nt + i, 0)),
        out_shape=jax.ShapeDtypeStruct((batch * seq, D_MODEL), BF16),
        scratch_shapes=[pltpu.VMEM((ts + DW_HALO, D_MODEL), F32),
                        pltpu.VMEM((SUBLANES - 1, ts + DW_SHIFT_EXTRA, D_MODEL), F32),
                        pltpu.VMEM((ts, D_MODEL), F32)],
        compiler_params=_params(2),
        name="dwconv",
    )(u, w_rep, b_dw.reshape(1, D_MODEL), ln_g.reshape(1, D_MODEL), ln_b.reshape(1, D_MODEL))


def _split3(v):
    h1 = v.astype(BF16)
    r1 = v - h1.astype(F32)
    h2 = r1.astype(BF16)
    r2 = r1 - h2.astype(F32)
    return h1, h2, r2.astype(BF16)


def _ssd_kernel(xbc_ref, z_ref, dt_ref, wc_ref, bc_ref, dtb_ref, alog_ref, d_ref, ng_ref, sel_ref,
                o_ref, win_ref, xc_ref, st_ref):
    L = MB_CHUNK
    rows_step = xbc_ref.shape[0]
    c = pl.program_id(1)

    @pl.when(c == 0)
    def _():
        win_ref[0:MB_CARRY, :] = jnp.zeros((MB_CARRY, MB_XBC), BF16)
        st_ref[...] = jnp.zeros(st_ref.shape, F32)

    @pl.when(c > 0)
    def _():
        win_ref[0:MB_CARRY, :] = win_ref[rows_step:rows_step + MB_CARRY, :]

    win_ref[MB_CARRY:MB_CARRY + rows_step, :] = xbc_ref[...]

    for r0 in range(0, rows_step, L):
        _ssd_chunk(r0, xbc_ref, z_ref, dt_ref, wc_ref, bc_ref, dtb_ref, alog_ref, d_ref, ng_ref,
                   sel_ref, o_ref, win_ref, xc_ref, st_ref)


def _ssd_chunk(r0, xbc_ref, z_ref, dt_ref, wc_ref, bc_ref, dtb_ref, alog_ref, d_ref, ng_ref,
               sel_ref, o_ref, win_ref, xc_ref, st_ref):
    L = MB_CHUNK
    rows_c = slice(r0, r0 + L)
    cw = 512
    for c0 in range(0, MB_XBC, cw):
        delayed = _dot(sel_ref[...], win_ref[r0:r0 + MB_CARRY + L, c0:c0 + cw])
        acc = bc_ref[:, c0:c0 + cw] + xbc_ref[rows_c, c0:c0 + cw].astype(F32) * wc_ref[MB_CONV - 1:MB_CONV, c0:c0 + cw]
        for k in range(MB_CONV - 1):
            acc = acc + delayed[k * L:(k + 1) * L, :] * wc_ref[k:k + 1, c0:c0 + cw]
        xc_ref[rows_c, c0:c0 + cw] = _silu(acc)

    dtr = dt_ref[rows_c, :] + dtb_ref[...]
    dt = jnp.maximum(dtr, 0.0) + jnp.log1p(jnp.exp(-jnp.abs(dtr)))
    dta = dt * (-LOG2_E * jnp.exp(alog_ref[...]))
    dt_t = dt.T
    dta_t = dta.T
    rows = lax.broadcasted_iota(jnp.int32, (L, L), 0)
    cols = lax.broadcasted_iota(jnp.int32, (L, L), 1)
    causal = rows >= cols
    tril = jnp.where(causal, 1.0, 0.0).astype(BF16)
    triu = jnp.where(rows <= cols, 1.0, 0.0).astype(BF16)
    c1, c2, c3 = _split3(dta)
    acol = _dot(tril, c1) + _dot(tril, c2) + _dot(tril, c3)
    r1, r2, r3 = _split3(dta_t)
    arow = _dot(r1, triu) + _dot(r2, triu) + _dot(r3, triu)
    alast = arow[:, L - 1:L]
    state_decay = jnp.broadcast_to(jnp.exp2(alast), (LANES, LANES))
    w_in_state = dt_t * jnp.exp2(alast - arow)

    lane2 = lax.broadcasted_iota(jnp.int32, (1, MB_GROUP_DIM), 1)
    for g in range(MB_GROUPS):
        x0 = g * MB_GROUP_DIM
        b0 = MB_D_INNER + g * MB_D_STATE
        c0 = MB_D_INNER + MB_GROUPS * MB_D_STATE + g * MB_D_STATE
        bg = xc_ref[rows_c, b0:b0 + MB_D_STATE]
        cg = xc_ref[rows_c, c0:c0 + MB_D_STATE]
        xg = xc_ref[rows_c, x0:x0 + MB_GROUP_DIM]
        xgb = xg.astype(BF16)
        cb = _dot_nt(cg.astype(BF16), bg.astype(BF16))
        bg_t = bg.T
        h_old = st_ref[g]
        rhs = jnp.concatenate([xgb, h_old.astype(BF16)], axis=0)
        h_decay = jnp.zeros((1, MB_GROUP_DIM), F32)
        out_lhs, state_lhs, heads = [], [], []
        for j in range(MB_HPG):
            hh = g * MB_HPG + j
            head = (lane2 >= j * MB_HEAD_DIM) & (lane2 < (j + 1) * MB_HEAD_DIM)
            a_l = jnp.broadcast_to(acol[:, hh:hh + 1], (L, L))
            seg = a_l - arow[hh:hh + 1, :]
            decay = jnp.exp2(jnp.where(causal, seg, -jnp.inf))
            wts = cb * decay * dt_t[hh:hh + 1, :]
            cs = cg * jnp.exp2(a_l)
            out_lhs.append(jnp.concatenate([wts, cs], axis=1).astype(BF16))
            state_lhs.append((bg_t * w_in_state[hh:hh + 1, :]).astype(BF16))
            sd = state_decay[hh:hh + 1, :]
            h_decay = h_decay + jnp.where(head, jnp.concatenate([sd, sd], axis=1), 0.0)
            heads.append(head)
        y_all = _dot(jnp.concatenate(out_lhs, axis=0), rhs)
        h_all = _dot(jnp.concatenate(state_lhs, axis=0), xgb)
        y = jnp.zeros((L, MB_GROUP_DIM), F32)
        h_new = jnp.zeros((MB_D_STATE, MB_GROUP_DIM), F32)
        for j, head in enumerate(heads):
            y = jnp.where(head, y_all[j * L:(j + 1) * L, :], y)
            h_new = jnp.where(head, h_all[j * MB_D_STATE:(j + 1) * MB_D_STATE, :], h_new)
        st_ref[g] = h_old * h_decay + h_new

        y = y + d_ref[:, x0:x0 + MB_GROUP_DIM] * xg
        y = y * _silu(z_ref[rows_c, x0:x0 + MB_GROUP_DIM].astype(F32))
        y = y * lax.rsqrt(jnp.mean(y * y, axis=-1, keepdims=True) + 1e-5)
        o_ref[rows_c, x0:x0 + MB_GROUP_DIM] = (y * ng_ref[:, x0:x0 + MB_GROUP_DIM]).astype(o_ref.dtype)


def _ssd_scan(xz, dt_raw, w_conv, b_conv, dt_bias, a_log, d_skip, norm_g, batch, seq):
    L = MB_CHUNK
    rows_step = MB_CHUNKS_PER_STEP * L
    assert seq % rows_step == 0
    nc = seq // rows_step
    pad = LANES - MB_HEADS
    dtb = jnp.pad(dt_bias, (0, pad)).reshape(1, LANES)
    alog = jnp.pad(a_log, (0, pad)).reshape(1, LANES)
    d_chan = jnp.repeat(d_skip, MB_HEAD_DIM).reshape(1, MB_D_INNER)
    fixed = lambda b, c: (0, 0)
    z_col = MB_XBC // MB_D_INNER
    sel = np.zeros((MB_CONV - 1, L, MB_CARRY + L), np.float32)
    for k in range(MB_CONV - 1):
        sel[k, np.arange(L), np.arange(L) + MB_CARRY - (MB_CONV - 1) + k] = 1.0
    sel = jnp.asarray(sel.reshape(-1, MB_CARRY + L), BF16)
    return pl.pallas_call(
        _ssd_kernel,
        grid=(batch, nc),
        in_specs=[pl.BlockSpec((rows_step, MB_XBC), lambda b, c: (b * nc + c, 0)),
                  pl.BlockSpec((rows_step, MB_D_INNER), lambda b, c: (b * nc + c, z_col)),
                  pl.BlockSpec((rows_step, LANES), lambda b, c: (b * nc + c, 0)),
                  pl.BlockSpec((MB_CONV, MB_XBC), fixed), pl.BlockSpec((1, MB_XBC), fixed),
                  pl.BlockSpec((1, LANES), fixed), pl.BlockSpec((1, LANES), fixed),
                  pl.BlockSpec((1, MB_D_INNER), fixed), pl.BlockSpec((1, MB_D_INNER), fixed),
                  pl.BlockSpec(sel.shape, fixed)],
        out_specs=pl.BlockSpec((rows_step, MB_D_INNER), lambda b, c: (b * nc + c, 0)),
        out_shape=jax.ShapeDtypeStruct((batch * seq, MB_D_INNER), BF16),
        scratch_shapes=[pltpu.VMEM((MB_CARRY + rows_step, MB_XBC), BF16),
                        pltpu.VMEM((rows_step, MB_XBC), F32),
                        pltpu.VMEM((MB_GROUPS, MB_D_STATE, MB_GROUP_DIM), F32)],
        compiler_params=_params(2),
        name="ssd_scan",
    )(xz, xz, dt_raw, w_conv, b_conv.reshape(1, MB_XBC), dtb, alog, d_chan,
      norm_g.reshape(1, MB_D_INNER), sel)


def _ssd_mixer(xb, w_in, w_conv, b_conv, dt_bias, a_log, d_skip, norm_g, batch, seq):
    w_z = w_in[:, :MB_D_INNER]
    w_xbc = w_in[:, MB_D_INNER:MB_D_INNER + MB_XBC]
    w_dt = jnp.pad(w_in[:, MB_D_INNER + MB_XBC:], ((0, 0), (0, LANES - MB_HEADS)))
    xz = _mm(xb, jnp.concatenate([w_xbc, w_z], axis=1).astype(BF16), BF16, 1024, 2048)
    dt_raw = _mm(xb, w_dt.astype(BF16), F32, 1024, LANES)
    return _ssd_scan(xz, dt_raw, w_conv, b_conv, dt_bias, a_log, d_skip, norm_g, batch, seq)


def _qkv_kernel(x_ref, w_ref, pos_ref, inv_ref, m1_ref, m2_ref, o_ref):
    x = x_ref[...]
    ang = pos_ref[...] * inv_ref[...]
    cos = jnp.cos(ang)
    sin = jnp.sin(ang)
    s_up = sin * m1_ref[...]
    s_dn = sin * m2_ref[...]
    half = ROPE_DIM // 2
    for c0 in range(0, 2 * D_MODEL, 2 * LANES):
        y = _dot(x, w_ref[:, c0:c0 + 2 * LANES])
        for v0 in range(0, 2 * LANES, LANES):
            blk = y[:, v0:v0 + LANES]
            rot = (blk * cos + pltpu.roll(blk, LANES - half, 1) * s_up
                   + pltpu.roll(blk, half, 1) * s_dn)
            if c0 < D_MODEL:
                rot = rot * LOG2_E
            o_ref[:, c0 + v0:c0 + v0 + LANES] = rot.astype(o_ref.dtype)
    o_ref[:, 2 * D_MODEL:] = _dot(x, w_ref[:, 2 * D_MODEL:]).astype(o_ref.dtype)


def _qkv_rope(xb, w_qkv, pos):
    t = xb.shape[0]
    tm = _row_tile(t, 512)
    half = ROPE_DIM // 2
    lane = jnp.arange(LANES) % DA_HEAD_DIM
    inv = ROPE_THETA ** (-jnp.arange(0, ROPE_DIM, 2, dtype=F32) / ROPE_DIM)
    inv_lane = jnp.where(lane < ROPE_DIM, inv[lane % half], 0.0).astype(F32).reshape(1, LANES)
    m1 = jnp.where(lane < half, -1.0, 0.0).astype(F32).reshape(1, LANES)
    m2 = jnp.where((lane >= half) & (lane < ROPE_DIM), 1.0, 0.0).astype(F32).reshape(1, LANES)
    scale = jnp.concatenate([jnp.full((D_MODEL,), DA_HEAD_DIM ** -0.5, F32),
                             jnp.ones((2 * D_MODEL,), F32)])
    w = (w_qkv * scale).astype(BF16)
    fixed = lambda i: (0, 0)
    return pl.pallas_call(
        _qkv_kernel,
        grid=(t // tm,),
        in_specs=[pl.BlockSpec((tm, D_MODEL), lambda i: (i, 0)),
                  pl.BlockSpec((D_MODEL, 3 * D_MODEL), fixed, pipeline_mode=pl.Buffered(1)),
                  pl.BlockSpec((tm, 1), lambda i: (i, 0)),
                  pl.BlockSpec((1, LANES), fixed), pl.BlockSpec((1, LANES), fixed),
                  pl.BlockSpec((1, LANES), fixed)],
        out_specs=pl.BlockSpec((tm, 3 * D_MODEL), lambda i: (i, 0)),
        out_shape=jax.ShapeDtypeStruct((t, 3 * D_MODEL), BF16),
        compiler_params=_params(1),
        name="qkv_rope",
    )(xb, w, pos, inv_lane, m1, m2)


def _diff_attn_kernel(q_ref, k_ref, v_ref, lq1_ref, lk1_ref, lq2_ref, lk2_ref, g_ref, o_ref,
                      vt_ref, *, tq, seq, lambda_init):
    lane = lax.broadcasted_iota(jnp.int32, (1, DA_V_DIM), 1)
    lam = (jnp.exp(jnp.sum(lq1_ref[...] * lk1_ref[...], axis=-1, keepdims=True))
           - jnp.exp(jnp.sum(lq2_ref[...] * lk2_ref[...], axis=-1, keepdims=True)) + lambda_init)
    visible = (lax.broadcasted_iota(jnp.int32, (tq, tq), 0)
               <= lax.broadcasted_iota(jnp.int32, (tq, tq), 1))
    vt_ref[...] = v_ref[...].astype(F32).T.astype(BF16)
    for q0 in range(0, seq, tq):
        q = q_ref[q0:q0 + tq, :]
        zero = jnp.zeros_like(q)
        comps = (jnp.where(lane < DA_HEAD_DIM, q, zero), jnp.where(lane >= DA_HEAD_DIM, q, zero))
        outs = []
        for qc in comps:
            s_d = jnp.where(visible, _dot_nt(k_ref[q0:q0 + tq, :], qc), -jnp.inf)
            m = jnp.max(s_d, axis=0, keepdims=True)
            if q0 > 0:
                s_p = _dot_nt(k_ref[0:q0, :], qc)
                m = jnp.maximum(m, jnp.max(s_p, axis=0, keepdims=True))
            p_d = jnp.exp2(s_d - m)
            l = jnp.sum(p_d, axis=0, keepdims=True)
            acc = _dot(vt_ref[:, q0:q0 + tq], p_d.astype(BF16))
            if q0 > 0:
                p_p = jnp.exp2(s_p - m)
                l = l + jnp.sum(p_p, axis=0, keepdims=True)
                acc = acc + _dot(vt_ref[:, 0:q0], p_p.astype(BF16))
            outs.append(acc / l)
        o = (outs[0] - lam * outs[1]).T
        o = o * lax.rsqrt(jnp.mean(o * o, axis=-1, keepdims=True) + 1e-5) * g_ref[...]
        o_ref[q0:q0 + tq, :] = (o * (1.0 - lambda_init)).astype(o_ref.dtype)


def _diff_attn(qkv, lq1, lk1, lq2, lk2, subln_g, lambda_init, batch, seq):
    tq = _row_tile(seq, 512)
    vec = lambda a: jnp.pad(a, (0, LANES - a.shape[0])).reshape(1, LANES)
    fixed = lambda b, h: (0, 0)
    return pl.pallas_call(
        functools.partial(_diff_attn_kernel, tq=tq, seq=seq, lambda_init=lambda_init),
        grid=(batch, DA_HEADS),
        in_specs=[pl.BlockSpec((seq, DA_V_DIM), lambda b, h: (b, h)),
                  pl.BlockSpec((seq, DA_V_DIM), lambda b, h: (b, DA_HEADS + h)),
                  pl.BlockSpec((seq, DA_V_DIM), lambda b, h: (b, 2 * DA_HEADS + h)),
                  pl.BlockSpec((1, LANES), fixed), pl.BlockSpec((1, LANES), fixed),
                  pl.BlockSpec((1, LANES), fixed), pl.BlockSpec((1, LANES), fixed),
                  pl.BlockSpec((1, DA_V_DIM), fixed)],
        out_specs=pl.BlockSpec((seq, DA_V_DIM), lambda b, h: (b, h)),
        out_shape=jax.ShapeDtypeStruct((batch * seq, D_MODEL), BF16),
        scratch_shapes=[pltpu.VMEM((DA_V_DIM, seq), BF16)],
        compiler_params=_params(2),
        name="diff_attn",
    )(qkv, qkv, qkv, vec(lq1), vec(lk1), vec(lq2), vec(lk2), subln_g.reshape(1, DA_V_DIM))


def _xa_kv_kernel(mem_ref, wkt_ref, wv_ref, kt_ref, v_ref):
    m = mem_ref[0]
    kt_ref[0] = _dot_nt(wkt_ref[...], m).astype(kt_ref.dtype)
    v_ref[0] = _dot(m, wv_ref[...]).astype(v_ref.dtype)


def _xa_kv(memb, w_kv):
    batch, n_mem, _ = memb.shape
    wkt = w_kv[:, :D_MODEL].T.astype(BF16)
    wv = w_kv[:, D_MODEL:].astype(BF16)
    return pl.pallas_call(
        _xa_kv_kernel,
        grid=(batch,),
        in_specs=[pl.BlockSpec((1, n_mem, D_MODEL), lambda b: (b, 0, 0)),
                  pl.BlockSpec((D_MODEL, D_MODEL), lambda b: (0, 0)),
                  pl.BlockSpec((D_MODEL, D_MODEL), lambda b: (0, 0))],
        out_specs=[pl.BlockSpec((1, D_MODEL, n_mem), lambda b: (b, 0, 0)),
                   pl.BlockSpec((1, n_mem, D_MODEL), lambda b: (b, 0, 0))],
        out_shape=[jax.ShapeDtypeStruct((batch, D_MODEL, n_mem), BF16),
                   jax.ShapeDtypeStruct((batch, n_mem, D_MODEL), BF16)],
        compiler_params=_params(1),
        name="xa_kv",
    )(memb, wkt, wv)


def _xattn_kernel(h_ref, wm_ref, bm_ref, xf_ref, g0_ref, b0_ref, wq_ref, kt_ref, v_ref, wo_ref,
                  g1_ref, b1_ref, of_ref, ob_ref):
    y0 = _dot(h_ref[...], wm_ref[...]) + bm_ref[...]
    x1 = _layer_norm(DN_ALPHA * xf_ref[...] + y0, g0_ref[...], b0_ref[...])
    q = _dot(x1.astype(BF16), wq_ref[...]).astype(BF16)
    heads = []
    for h in range(XA_HEADS):
        sl = slice(h * XA_HEAD_DIM, (h + 1) * XA_HEAD_DIM)
        s = _dot(q[:, sl], kt_ref[0, sl, :])
        p = jnp.exp(s - jnp.max(s, axis=-1, keepdims=True))
        o = _dot(p.astype(BF16), v_ref[0, :, sl]) / jnp.sum(p, axis=-1, keepdims=True)
        heads.append(o.astype(BF16))
    o_all = jnp.concatenate(heads, axis=1)
    half = o_all.shape[0] // 2
    for r0 in (0, half):
        rows = slice(r0, r0 + half)
        y = _dot(o_all[rows, :], wo_ref[...])
        out = _layer_norm(DN_ALPHA * x1[rows, :] + y, g1_ref[...], b1_ref[...])
        of_ref[rows, :] = out
        ob_ref[rows, :] = out.astype(BF16)


def _mixer_out_xattn(h, w_mix, b_mix, xf, g0, b0, memb, w_q, w_kv, w_out, g1, b1, batch, seq):
    kt, v = _xa_kv(memb, w_kv)
    n_mem = memb.shape[1]
    k_mix = h.shape[1]
    tm = _row_tile(seq, 512)
    nt = seq // tm
    wq = (w_q * (XA_HEAD_DIM ** -0.5)).astype(BF16)
    vec = lambda a: a.reshape(1, D_MODEL)
    row = lambda bi, i: (bi * nt + i, 0)
    fixed = lambda bi, i: (0, 0)
    resident = pl.Buffered(1)
    return pl.pallas_call(
        _xattn_kernel,
        grid=(batch, nt),
        in_specs=[pl.BlockSpec((tm, k_mix), row),
                  pl.BlockSpec((k_mix, D_MODEL), fixed, pipeline_mode=resident),
                  pl.BlockSpec((1, D_MODEL), fixed), pl.BlockSpec((tm, D_MODEL), row),
                  pl.BlockSpec((1, D_MODEL), fixed), pl.BlockSpec((1, D_MODEL), fixed),
                  pl.BlockSpec((D_MODEL, D_MODEL), fixed, pipeline_mode=resident),
                  pl.BlockSpec((1, D_MODEL, n_mem), lambda bi, i: (bi, 0, 0)),
                  pl.BlockSpec((1, n_mem, D_MODEL), lambda bi, i: (bi, 0, 0)),
                  pl.BlockSpec((D_MODEL, D_MODEL), fixed, pipeline_mode=resident),
                  pl.BlockSpec((1, D_MODEL), fixed), pl.BlockSpec((1, D_MODEL), fixed)],
        out_specs=[pl.BlockSpec((tm, D_MODEL), row), pl.BlockSpec((tm, D_MODEL), row)],
        out_shape=[jax.ShapeDtypeStruct((batch * seq, D_MODEL), F32),
                   jax.ShapeDtypeStruct((batch * seq, D_MODEL), BF16)],
        compiler_params=_params(2),
        name="mixer_out_xattn",
    )(h, w_mix.astype(BF16), vec(b_mix), xf, vec(g0), vec(b0), wq, kt, v, w_out.astype(BF16),
      vec(g1), vec(b1))


FF_CHUNK = 256
FF_ROW_BLOCKS = 2


def _swiglu_kernel(xb_ref, xf_ref, wg_ref, wu_ref, wo_ref, g_ref, b_ref, of_ref, ob_ref, h_ref):
    xb = xb_ref[...]
    for c0 in range(0, D_FF, FF_CHUNK):
        gate = _dot(xb, wg_ref[:, c0:c0 + FF_CHUNK])
        up = _dot(xb, wu_ref[:, c0:c0 + FF_CHUNK])
        h_ref[:, c0:c0 + FF_CHUNK] = (_silu(gate) * up).astype(BF16)
    block = h_ref.shape[0] // FF_ROW_BLOCKS
    for r0 in range(0, h_ref.shape[0], block):
        rows = slice(r0, r0 + block)
        y = _dot(h_ref[rows, :], wo_ref[...])
        out = _layer_norm(DN_ALPHA * xf_ref[rows, :] + y, g_ref[...], b_ref[...])
        of_ref[rows, :] = out
        ob_ref[rows, :] = out.astype(BF16)


def _swiglu(xf, xb, w_in, w_out, g, b):
    t = xf.shape[0]
    tm = _row_tile(t, 512)
    row = lambda i: (i, 0)
    fixed = lambda i: (0, 0)
    resident = pl.Buffered(1)
    w_in_b = w_in.astype(BF16)
    return pl.pallas_call(
        _swiglu_kernel,
        grid=(t // tm,),
        in_specs=[pl.BlockSpec((tm, D_MODEL), row), pl.BlockSpec((tm, D_MODEL), row),
                  pl.BlockSpec((D_MODEL, D_FF), lambda i: (0, 0), pipeline_mode=resident),
                  pl.BlockSpec((D_MODEL, D_FF), lambda i: (0, 1), pipeline_mode=resident),
                  pl.BlockSpec((D_FF, D_MODEL), fixed, pipeline_mode=resident),
                  pl.BlockSpec((1, D_MODEL), fixed), pl.BlockSpec((1, D_MODEL), fixed)],
        out_specs=[pl.BlockSpec((tm, D_MODEL), row), pl.BlockSpec((tm, D_MODEL), row)],
        out_shape=[jax.ShapeDtypeStruct((t, D_MODEL), F32),
                   jax.ShapeDtypeStruct((t, D_MODEL), BF16)],
        scratch_shapes=[pltpu.VMEM((tm, D_FF), BF16)],
        compiler_params=_params(1),
        name="swiglu",
    )(xb, xf, w_in_b, w_in_b, w_out.astype(BF16), g.reshape(1, D_MODEL), b.reshape(1, D_MODEL))


def kernel(x, mem, positions, cv_w_in, cv_b_in, cv_w_dw, cv_b_dw, cv_ln_g, cv_ln_b, cv_w_out, cv_b_out, mb_w_in, mb_w_conv, mb_b_conv, mb_dt_bias, mb_a_log, mb_d, mb_norm_g, mb_w_out, da_w_qkv, da_lq1, da_lk1, da_lq2, da_lk2, da_subln_g, da_w_out, xa_w_q, xa_w_kv, xa_w_out, ff_w_in, ff_w_out, ln_g, ln_b):
    batch, seq, _ = x.shape
    t = batch * seq
    xf = x.reshape(t, D_MODEL)
    xb = None
    memb = mem.astype(BF16)
    pos = positions.reshape(t, 1).astype(F32)
    no_bias = jnp.zeros((D_MODEL,), F32)
    for i in range(DEPTH):
        mixer, j = i % N_MIXERS, i // N_MIXERS
        if mixer == 0:
            u = _glu(xf if xb is None else xb, cv_w_in[j].astype(BF16), cv_b_in[j])
            h = _dwconv(u, cv_w_dw[j], cv_b_dw[j], cv_ln_g[j], cv_ln_b[j], batch, seq)
            w_mix, b_mix = cv_w_out[j], cv_b_out[j]
        elif mixer == 1:
            h = _ssd_mixer(xb, mb_w_in[j], mb_w_conv[j], mb_b_conv[j], mb_dt_bias[j],
                           mb_a_log[j], mb_d[j], mb_norm_g[j], batch, seq)
            w_mix, b_mix = mb_w_out[j], no_bias
        else:
            lambda_init = 0.8 - 0.6 * math.exp(-0.3 * i)
            qkv = _qkv_rope(xb, da_w_qkv[j], pos)
            h = _diff_attn(qkv, da_lq1[j], da_lk1[j], da_lq2[j], da_lk2[j], da_subln_g[j],
                           lambda_init, batch, seq)
            w_mix, b_mix = da_w_out[j], no_bias
        xf, xb = _mixer_out_xattn(h, w_mix, b_mix, xf, ln_g[i, 0], ln_b[i, 0], memb, xa_w_q[i],
                                  xa_w_kv[i], xa_w_out[i], ln_g[i, 1], ln_b[i, 1], batch, seq)
        xf, xb = _swiglu(xf, xb, ff_w_in[i], ff_w_out[i], ln_g[i, 2], ln_b[i, 2])
    return xf.reshape(batch, seq, D_MODEL)
```

```python
import functools
import math

import jax
import jax.numpy as jnp
import numpy as np
from jax import lax
from jax.experimental import pallas as pl
from jax.experimental.pallas import tpu as pltpu

F32 = jnp.float32
BF16 = jnp.bfloat16

D_MODEL = 1024
DEPTH = 4
N_MIXERS = 3
DN_ALPHA = (2.0 * DEPTH) ** 0.25
LN_EPS = 1e-5
CONV_WIDTH = 31
MB_D_INNER = 2 * D_MODEL
MB_HEAD_DIM = 64
MB_HEADS = MB_D_INNER // MB_HEAD_DIM
MB_D_STATE = 128
MB_GROUPS = 8
MB_HPG = MB_HEADS // MB_GROUPS
MB_GROUP_DIM = MB_D_INNER // MB_GROUPS
MB_CONV = 4
MB_CHUNK = 128
MB_CARRY = 16
MB_CHUNKS_PER_STEP = 4
MB_XBC = MB_D_INNER + 2 * MB_GROUPS * MB_D_STATE
DA_HEAD_DIM = 64
DA_HEADS = D_MODEL // (2 * DA_HEAD_DIM)
DA_V_DIM = 2 * DA_HEAD_DIM
ROPE_THETA = 500000.0
ROPE_DIM = DA_HEAD_DIM // 4
LOG2_E = math.log2(math.e)
XA_HEADS = 4
XA_HEAD_DIM = D_MODEL // XA_HEADS
D_FF = ((8 * D_MODEL + 3 * 256 - 1) // (3 * 256)) * 256

LANES = 128
SUBLANES = 8
VMEM_LIMIT_BYTES = 48 * 1024 * 1024


def _params(n_axes):
    return pltpu.CompilerParams(dimension_semantics=("arbitrary",) * n_axes,
                                vmem_limit_bytes=VMEM_LIMIT_BYTES)


def _row_tile(n_rows, want):
    t = min(n_rows, want)
    assert n_rows % t == 0, (n_rows, t)
    return t


def _dot(a, b):
    return jnp.dot(a, b, preferred_element_type=F32)


def _dot_nt(a, b):
    return lax.dot_general(a, b, (((1,), (1,)), ((), ())), preferred_element_type=F32)


def _layer_norm(r, g, b):
    mu = jnp.mean(r, axis=-1, keepdims=True)
    c = r - mu
    var = jnp.mean(c * c, axis=-1, keepdims=True)
    return c * lax.rsqrt(var + LN_EPS) * g + b


def _sigmoid(v):
    return 0.5 * jnp.tanh(0.5 * v) + 0.5


def _silu(v):
    h = 0.5 * v
    return h * jnp.tanh(h) + h


def _mm_kernel(x_ref, w_ref, o_ref):
    o_ref[...] = _dot(x_ref[...], w_ref[...]).astype(o_ref.dtype)


def _mm(x, w, out_dtype, tm, tn):
    t, k = x.shape
    n = w.shape[1]
    tm = _row_tile(t, tm)
    tn = _row_tile(n, tn)
    return pl.pallas_call(
        _mm_kernel,
        grid=(t // tm, n // tn),
        in_specs=[pl.BlockSpec((tm, k), lambda i, j: (i, 0)),
                  pl.BlockSpec((k, tn), lambda i, j: (0, j))],
        out_specs=pl.BlockSpec((tm, tn), lambda i, j: (i, j)),
        out_shape=jax.ShapeDtypeStruct((t, n), out_dtype),
        compiler_params=_params(2),
        name="mm",
    )(x, w)


def _glu_kernel(x_ref, w_ref, b_ref, o_ref):
    h = _dot(x_ref[...].astype(BF16), w_ref[...]) + b_ref[...]
    o_ref[...] = (h[:, :D_MODEL] * _sigmoid(h[:, D_MODEL:])).astype(o_ref.dtype)


def _glu(x, w, b):
    t = x.shape[0]
    tm = _row_tile(t, 1024)
    return pl.pallas_call(
        _glu_kernel,
        grid=(t // tm,),
        in_specs=[pl.BlockSpec((tm, D_MODEL), lambda i: (i, 0)),
                  pl.BlockSpec((D_MODEL, 2 * D_MODEL), lambda i: (0, 0)),
                  pl.BlockSpec((1, 2 * D_MODEL), lambda i: (0, 0))],
        out_specs=pl.BlockSpec((tm, D_MODEL), lambda i: (i, 0)),
        out_shape=jax.ShapeDtypeStruct((t, D_MODEL), BF16),
        compiler_params=_params(1),
        name="glu",
    )(x, w, b.reshape(1, 2 * D_MODEL))


DW_HALO = 32
DW_ROWS = 64
DW_COLS = 256
DW_FIRST = DW_HALO - (CONV_WIDTH - 1)
DW_SHIFT_EXTRA = ((DW_FIRST + CONV_WIDTH - 2) // SUBLANES) * SUBLANES


def _dwconv_kernel(u_ref, w_ref, bdw_ref, g_ref, b_ref, o_ref, win_ref, sh_ref, acc_ref, *, ts):
    i = pl.program_id(1)

    @pl.when(i == 0)
    def _():
        win_ref[0:DW_HALO, :] = jnp.zeros((DW_HALO, D_MODEL), F32)

    @pl.when(i > 0)
    def _():
        win_ref[0:DW_HALO, :] = win_ref[ts:ts + DW_HALO, :]

    win_ref[DW_HALO:DW_HALO + ts, :] = u_ref[...].astype(F32)

    n_sh = ts + DW_SHIFT_EXTRA
    for r in range(1, SUBLANES):
        for c0 in range(0, D_MODEL, DW_COLS):
            sh_ref[r - 1, :, c0:c0 + DW_COLS] = win_ref[r:r + n_sh, c0:c0 + DW_COLS]

    groups = (DW_ROWS // SUBLANES, SUBLANES, DW_COLS)
    for r0 in range(0, ts, DW_ROWS):
        for c0 in range(0, D_MODEL, DW_COLS):
            acc = jnp.zeros(groups, F32)
            for k in range(CONV_WIDTH):
                off = DW_FIRST + k
                r = off % SUBLANES
                a = r0 + off - r
                if r == 0:
                    src = win_ref[a:a + DW_ROWS, c0:c0 + DW_COLS]
                else:
                    src = sh_ref[r - 1, a:a + DW_ROWS, c0:c0 + DW_COLS]
                wk = w_ref[k * SUBLANES:(k + 1) * SUBLANES, c0:c0 + DW_COLS]
                acc = acc + src.reshape(groups) * wk
            acc_ref[r0:r0 + DW_ROWS, c0:c0 + DW_COLS] = acc.reshape(DW_ROWS, DW_COLS)

    v = acc_ref[...] + bdw_ref[...]
    o_ref[...] = _silu(_layer_norm(v, g_ref[...], b_ref[...])).astype(o_ref.dtype)


def _dwconv(u, w_dw, b_dw, ln_g, ln_b, batch, seq):
    ts = _row_tile(seq, 256)
    nt = seq // ts
    w_rep = jnp.repeat(w_dw, SUBLANES, axis=0)
    fixed = lambda b, i: (0, 0)
    return pl.pallas_call(
        functools.partial(_dwconv_kernel, ts=ts),
        grid=(batch, nt),
        in_specs=[pl.BlockSpec((ts, D_MODEL), lambda b, i: (b * nt + i, 0)),
                  pl.BlockSpec((CONV_WIDTH * SUBLANES, D_MODEL), fixed),
                  pl.BlockSpec((1, D_MODEL), fixed), pl.BlockSpec((1, D_MODEL), fixed),
                  pl.BlockSpec((1, D_MODEL), fixed)],
        out_specs=pl.BlockSpec((ts, D_MODEL), lambda b, i: (b * nt + i, 0)),
        out_shape=jax.ShapeDtypeStruct((batch * seq, D_MODEL), BF16),
        scratch_shapes=[pltpu.VMEM((ts + DW_HALO, D_MODEL), F32),
                        pltpu.VMEM((SUBLANES - 1, ts + DW_SHIFT_EXTRA, D_MODEL), F32),
                        pltpu.VMEM((ts, D_MODEL), F32)],
        compiler_params=_params(2),
        name="dwconv",
    )(u, w_rep, b_dw.reshape(1, D_MODEL), ln_g.reshape(1, D_MODEL), ln_b.reshape(1, D_MODEL))


def _split3(v):
    h1 = v.astype(BF16)
    r1 = v - h1.astype(F32)
    h2 = r1.astype(BF16)
    r2 = r1 - h2.astype(F32)
    return h1, h2, r2.astype(BF16)


def _ssd_kernel(xbc_ref, z_ref, dt_ref, wc_ref, bc_ref, dtb_ref, alog_ref, d_ref, ng_ref, sel_ref,
                o_ref, win_ref, xc_ref, st_ref):
    L = MB_CHUNK
    rows_step = xbc_ref.shape[0]
    c = pl.program_id(1)

    @pl.when(c == 0)
    def _():
        win_ref[0:MB_CARRY, :] = jnp.zeros((MB_CARRY, MB_XBC), BF16)
        st_ref[...] = jnp.zeros(st_ref.shape, F32)

    @pl.when(c > 0)
    def _():
        win_ref[0:MB_CARRY, :] = win_ref[rows_step:rows_step + MB_CARRY, :]

    win_ref[MB_CARRY:MB_CARRY + rows_step, :] = xbc_ref[...]

    for r0 in range(0, rows_step, L):
        _ssd_chunk(r0, xbc_ref, z_ref, dt_ref, wc_ref, bc_ref, dtb_ref, alog_ref, d_ref, ng_ref,
                   sel_ref, o_ref, win_ref, xc_ref, st_ref)


def _ssd_chunk(r0, xbc_ref, z_ref, dt_ref, wc_ref, bc_ref, dtb_ref, alog_ref, d_ref, ng_ref,
               sel_ref, o_ref, win_ref, xc_ref, st_ref):
    L = MB_CHUNK
    rows_c = slice(r0, r0 + L)
    cw = 512
    for c0 in range(0, MB_XBC, cw):
        delayed = _dot(sel_ref[...], win_ref[r0:r0 + MB_CARRY + L, c0:c0 + cw])
        acc = bc_ref[:, c0:c0 + cw] + xbc_ref[rows_c, c0:c0 + cw].astype(F32) * wc_ref[MB_CONV - 1:MB_CONV, c0:c0 + cw]
        for k in range(MB_CONV - 1):
            acc = acc + delayed[k * L:(k + 1) * L, :] * wc_ref[k:k + 1, c0:c0 + cw]
        xc_ref[rows_c, c0:c0 + cw] = _silu(acc)

    dtr = dt_ref[rows_c, :] + dtb_ref[...]
    dt = jnp.maximum(dtr, 0.0) + jnp.log1p(jnp.exp(-jnp.abs(dtr)))
    dta = dt * (-LOG2_E * jnp.exp(alog_ref[...]))
    dt_t = dt.T
    dta_t = dta.T
    rows = lax.broadcasted_iota(jnp.int32, (L, L), 0)
    cols = lax.broadcasted_iota(jnp.int32, (L, L), 1)
    causal = rows >= cols
    tril = jnp.where(causal, 1.0, 0.0).astype(BF16)
    triu = jnp.where(rows <= cols, 1.0, 0.0).astype(BF16)
    c1, c2, c3 = _split3(dta)
    acol = _dot(tril, c1) + _dot(tril, c2) + _dot(tril, c3)
    r1, r2, r3 = _split3(dta_t)
    arow = _dot(r1, triu) + _dot(r2, triu) + _dot(r3, triu)
    alast = arow[:, L - 1:L]
    state_decay = jnp.broadcast_to(jnp.exp2(alast), (LANES, LANES))
    w_in_state = dt_t * jnp.exp2(alast - arow)

    lane2 = lax.broadcasted_iota(jnp.int32, (1, MB_GROUP_DIM), 1)
    for g in range(MB_GROUPS):
        x0 = g * MB_GROUP_DIM
        b0 = MB_D_INNER + g * MB_D_STATE
        c0 = MB_D_INNER + MB_GROUPS * MB_D_STATE + g * MB_D_STATE
        bg = xc_ref[rows_c, b0:b0 + MB_D_STATE]
        cg = xc_ref[rows_c, c0:c0 + MB_D_STATE]
        xg = xc_ref[rows_c, x0:x0 + MB_GROUP_DIM]
        xgb = xg.astype(BF16)
        cb = _dot_nt(cg.astype(BF16), bg.astype(BF16))
        bg_t = bg.T
        h_old = st_ref[g]
        rhs = jnp.concatenate([xgb, h_old.astype(BF16)], axis=0)
        h_decay = jnp.zeros((1, MB_GROUP_DIM), F32)
        out_lhs, state_lhs, heads = [], [], []
        for j in range(MB_HPG):
            hh = g * MB_HPG + j
            head = (lane2 >= j * MB_HEAD_DIM) & (lane2 < (j + 1) * MB_HEAD_DIM)
            a_l = jnp.broadcast_to(acol[:, hh:hh + 1], (L, L))
            seg = a_l - arow[hh:hh + 1, :]
            decay = jnp.exp2(jnp.where(causal, seg, -jnp.inf))
            wts = cb * decay * dt_t[hh:hh + 1, :]
            cs = cg * jnp.exp2(a_l)
            out_lhs.append(jnp.concatenate([wts, cs], axis=1).astype(BF16))
            state_lhs.append((bg_t * w_in_state[hh:hh + 1, :]).astype(BF16))
            sd = state_decay[hh:hh + 1, :]
            h_decay = h_decay + jnp.where(head, jnp.concatenate([sd, sd], axis=1), 0.0)
            heads.append(head)
        y_all = _dot(jnp.concatenate(out_lhs, axis=0), rhs)
        h_all = _dot(jnp.concatenate(state_lhs, axis=0), xgb)
        y = jnp.zeros((L, MB_GROUP_DIM), F32)
        h_new = jnp.zeros((MB_D_STATE, MB_GROUP_DIM), F32)
        for j, head in enumerate(heads):
            y = jnp.where(head, y_all[j * L:(j + 1) * L, :], y)
            h_new = jnp.where(head, h_all[j * MB_D_STATE:(j + 1) * MB_D_STATE, :], h_new)
        st_ref[g] = h_old * h_decay + h_new

        y = y + d_ref[:, x0:x0 + MB_GROUP_DIM] * xg
        y = y * _silu(z_ref[rows_c, x0:x0 + MB_GROUP_DIM].astype(F32))
        y = y * lax.rsqrt(jnp.mean(y * y, axis=-1, keepdims=True) + 1e-5)
        o_ref[rows_c, x0:x0 + MB_GROUP_DIM] = (y * ng_ref[:, x0:x0 + MB_GROUP_DIM]).astype(o_ref.dtype)


def _ssd_scan(xz, dt_raw, w_conv, b_conv, dt_bias, a_log, d_skip, norm_g, batch, seq):
    L = MB_CHUNK
    rows_step = MB_CHUNKS_PER_STEP * L
    assert seq % rows_step == 0
    nc = seq // rows_step
    pad = LANES - MB_HEADS
    dtb = jnp.pad(dt_bias, (0, pad)).reshape(1, LANES)
    alog = jnp.pad(a_log, (0, pad)).reshape(1, LANES)
    d_chan = jnp.repeat(d_skip, MB_HEAD_DIM).reshape(1, MB_D_INNER)
    fixed = lambda b, c: (0, 0)
    z_col = MB_XBC // MB_D_INNER
    sel = np.zeros((MB_CONV - 1, L, MB_CARRY + L), np.float32)
    for k in range(MB_CONV - 1):
        sel[k, np.arange(L), np.arange(L) + MB_CARRY - (MB_CONV - 1) + k] = 1.0
    sel = jnp.asarray(sel.reshape(-1, MB_CARRY + L), BF16)
    return pl.pallas_call(
        _ssd_kernel,
        grid=(batch, nc),
        in_specs=[pl.BlockSpec((rows_step, MB_XBC), lambda b, c: (b * nc + c, 0)),
                  pl.BlockSpec((rows_step, MB_D_INNER), lambda b, c: (b * nc + c, z_col)),
                  pl.BlockSpec((rows_step, LANES), lambda b, c: (b * nc + c, 0)),
                  pl.BlockSpec((MB_CONV, MB_XBC), fixed), pl.BlockSpec((1, MB_XBC), fixed),
                  pl.BlockSpec((1, LANES), fixed), pl.BlockSpec((1, LANES), fixed),
                  pl.BlockSpec((1, MB_D_INNER), fixed), pl.BlockSpec((1, MB_D_INNER), fixed),
                  pl.BlockSpec(sel.shape, fixed)],
        out_specs=pl.BlockSpec((rows_step, MB_D_INNER), lambda b, c: (b * nc + c, 0)),
        out_shape=jax.ShapeDtypeStruct((batch * seq, MB_D_INNER), BF16),
        scratch_shapes=[pltpu.VMEM((MB_CARRY + rows_step, MB_XBC), BF16),
                        pltpu.VMEM((rows_step, MB_XBC), F32),
                        pltpu.VMEM((MB_GROUPS, MB_D_STATE, MB_GROUP_DIM), F32)],
        compiler_params=_params(2),
        name="ssd_scan",
    )(xz, xz, dt_raw, w_conv, b_conv.reshape(1, MB_XBC), dtb, alog, d_chan,
      norm_g.reshape(1, MB_D_INNER), sel)


def _ssd_mixer(xb, w_in, w_conv, b_conv, dt_bias, a_log, d_skip, norm_g, batch, seq):
    w_z = w_in[:, :MB_D_INNER]
    w_xbc = w_in[:, MB_D_INNER:MB_D_INNER + MB_XBC]
    w_dt = jnp.pad(w_in[:, MB_D_INNER + MB_XBC:], ((0, 0), (0, LANES - MB_HEADS)))
    xz = _mm(xb, jnp.concatenate([w_xbc, w_z], axis=1).astype(BF16), BF16, 1024, 2048)
    dt_raw = _mm(xb, w_dt.astype(BF16), F32, 1024, LANES)
    return _ssd_scan(xz, dt_raw, w_conv, b_conv, dt_bias, a_log, d_skip, norm_g, batch, seq)


def _qkv_kernel(x_ref, w_ref, pos_ref, inv_ref, m1_ref, m2_ref, o_ref):
    x = x_ref[...]
    ang = pos_ref[...] * inv_ref[...]
    cos = jnp.cos(ang)
    sin = jnp.sin(ang)
    s_up = sin * m1_ref[...]
    s_dn = sin * m2_ref[...]
    half = ROPE_DIM // 2
    for c0 in range(0, 2 * D_MODEL, 2 * LANES):
        y = _dot(x, w_ref[:, c0:c0 + 2 * LANES])
        for v0 in range(0, 2 * LANES, LANES):
            blk = y[:, v0:v0 + LANES]
            rot = (blk * cos + pltpu.roll(blk, LANES - half, 1) * s_up
                   + pltpu.roll(blk, half, 1) * s_dn)
            if c0 < D_MODEL:
                rot = rot * LOG2_E
            o_ref[:, c0 + v0:c0 + v0 + LANES] = rot.astype(o_ref.dtype)
    o_ref[:, 2 * D_MODEL:] = _dot(x, w_ref[:, 2 * D_MODEL:]).astype(o_ref.dtype)


def _qkv_rope(xb, w_qkv, pos):
    t = xb.shape[0]
    tm = _row_tile(t, 1024)
    half = ROPE_DIM // 2
    lane = jnp.arange(LANES) % DA_HEAD_DIM
    inv = ROPE_THETA ** (-jnp.arange(0, ROPE_DIM, 2, dtype=F32) / ROPE_DIM)
    inv_lane = jnp.where(lane < ROPE_DIM, inv[lane % half], 0.0).astype(F32).reshape(1, LANES)
    m1 = jnp.where(lane < half, -1.0, 0.0).astype(F32).reshape(1, LANES)
    m2 = jnp.where((lane >= half) & (lane < ROPE_DIM), 1.0, 0.0).astype(F32).reshape(1, LANES)
    scale = jnp.concatenate([jnp.full((D_MODEL,), DA_HEAD_DIM ** -0.5, F32),
                             jnp.ones((2 * D_MODEL,), F32)])
    w = (w_qkv * scale).astype(BF16)
    fixed = lambda i: (0, 0)
    return pl.pallas_call(
        _qkv_kernel,
        grid=(t // tm,),
        in_specs=[pl.BlockSpec((tm, D_MODEL), lambda i: (i, 0)),
                  pl.BlockSpec((D_MODEL, 3 * D_MODEL), fixed, pipeline_mode=pl.Buffered(1)),
                  pl.BlockSpec((tm, 1), lambda i: (i, 0)),
                  pl.BlockSpec((1, LANES), fixed), pl.BlockSpec((1, LANES), fixed),
                  pl.BlockSpec((1, LANES), fixed)],
        out_specs=pl.BlockSpec((tm, 3 * D_MODEL), lambda i: (i, 0)),
        out_shape=jax.ShapeDtypeStruct((t, 3 * D_MODEL), BF16),
        compiler_params=_params(1),
        name="qkv_rope",
    )(xb, w, pos, inv_lane, m1, m2)


def _diff_attn_kernel(q_ref, k_ref, v_ref, lq1_ref, lk1_ref, lq2_ref, lk2_ref, g_ref, o_ref,
                      vt_ref, *, tq, seq, lambda_init):
    lane = lax.broadcasted_iota(jnp.int32, (1, DA_V_DIM), 1)
    lam = (jnp.exp(jnp.sum(lq1_ref[...] * lk1_ref[...], axis=-1, keepdims=True))
           - jnp.exp(jnp.sum(lq2_ref[...] * lk2_ref[...], axis=-1, keepdims=True)) + lambda_init)
    visible = (lax.broadcasted_iota(jnp.int32, (tq, tq), 0)
               <= lax.broadcasted_iota(jnp.int32, (tq, tq), 1))
    vt_ref[...] = v_ref[...].astype(F32).T.astype(BF16)
    for q0 in range(0, seq, tq):
        q = q_ref[q0:q0 + tq, :]
        zero = jnp.zeros_like(q)
        comps = (jnp.where(lane < DA_HEAD_DIM, q, zero), jnp.where(lane >= DA_HEAD_DIM, q, zero))
        outs = []
        for qc in comps:
            s_d = jnp.where(visible, _dot_nt(k_ref[q0:q0 + tq, :], qc), -jnp.inf)
            m = jnp.max(s_d, axis=0, keepdims=True)
            if q0 > 0:
                s_p = _dot_nt(k_ref[0:q0, :], qc)
                m = jnp.maximum(m, jnp.max(s_p, axis=0, keepdims=True))
            p_d = jnp.exp2(s_d - m)
            l = jnp.sum(p_d, axis=0, keepdims=True)
            acc = _dot(vt_ref[:, q0:q0 + tq], p_d.astype(BF16))
            if q0 > 0:
                p_p = jnp.exp2(s_p - m)
                l = l + jnp.sum(p_p, axis=0, keepdims=True)
                acc = acc + _dot(vt_ref[:, 0:q0], p_p.astype(BF16))
            outs.append(acc / l)
        o = (outs[0] - lam * outs[1]).T
        o = o * lax.rsqrt(jnp.mean(o * o, axis=-1, keepdims=True) + 1e-5) * g_ref[...]
        o_ref[q0:q0 + tq, :] = (o * (1.0 - lambda_init)).astype(o_ref.dtype)


def _diff_attn(qkv, lq1, lk1, lq2, lk2, subln_g, lambda_init, batch, seq):
    tq = _row_tile(seq, 512)
    vec = lambda a: jnp.pad(a, (0, LANES - a.shape[0])).reshape(1, LANES)
    fixed = lambda b, h: (0, 0)
    return pl.pallas_call(
        functools.partial(_diff_attn_kernel, tq=tq, seq=seq, lambda_init=lambda_init),
        grid=(batch, DA_HEADS),
        in_specs=[pl.BlockSpec((seq, DA_V_DIM), lambda b, h: (b, h)),
                  pl.BlockSpec((seq, DA_V_DIM), lambda b, h: (b, DA_HEADS + h)),
                  pl.BlockSpec((seq, DA_V_DIM), lambda b, h: (b, 2 * DA_HEADS + h)),
                  pl.BlockSpec((1, LANES), fixed), pl.BlockSpec((1, LANES), fixed),
                  pl.BlockSpec((1, LANES), fixed), pl.BlockSpec((1, LANES), fixed),
                  pl.BlockSpec((1, DA_V_DIM), fixed)],
        out_specs=pl.BlockSpec((seq, DA_V_DIM), lambda b, h: (b, h)),
        out_shape=jax.ShapeDtypeStruct((batch * seq, D_MODEL), BF16),
        scratch_shapes=[pltpu.VMEM((DA_V_DIM, seq), BF16)],
        compiler_params=_params(2),
        name="diff_attn",
    )(qkv, qkv, qkv, vec(lq1), vec(lk1), vec(lq2), vec(lk2), subln_g.reshape(1, DA_V_DIM))


def _xa_kv_kernel(mem_ref, wkt_ref, wv_ref, kt_ref, v_ref):
    m = mem_ref[0]
    kt_ref[0] = _dot_nt(wkt_ref[...], m).astype(kt_ref.dtype)
    v_ref[0] = _dot(m, wv_ref[...]).astype(v_ref.dtype)


def _xa_kv(memb, w_kv):
    batch, n_mem, _ = memb.shape
    wkt = w_kv[:, :D_MODEL].T.astype(BF16)
    wv = w_kv[:, D_MODEL:].astype(BF16)
    return pl.pallas_call(
        _xa_kv_kernel,
        grid=(batch,),
        in_specs=[pl.BlockSpec((1, n_mem, D_MODEL), lambda b: (b, 0, 0)),
                  pl.BlockSpec((D_MODEL, D_MODEL), lambda b: (0, 0)),
                  pl.BlockSpec((D_MODEL, D_MODEL), lambda b: (0, 0))],
        out_specs=[pl.BlockSpec((1, D_MODEL, n_mem), lambda b: (b, 0, 0)),
                   pl.BlockSpec((1, n_mem, D_MODEL), lambda b: (b, 0, 0))],
        out_shape=[jax.ShapeDtypeStruct((batch, D_MODEL, n_mem), BF16),
                   jax.ShapeDtypeStruct((batch, n_mem, D_MODEL), BF16)],
        compiler_params=_params(1),
        name="xa_kv",
    )(memb, wkt, wv)


def _xattn_kernel(h_ref, wm_ref, bm_ref, xf_ref, g0_ref, b0_ref, wq_ref, kt_ref, v_ref, wo_ref,
                  g1_ref, b1_ref, of_ref, ob_ref):
    y0 = _dot(h_ref[...], wm_ref[...]) + bm_ref[...]
    x1 = _layer_norm(DN_ALPHA * xf_ref[...] + y0, g0_ref[...], b0_ref[...])
    q = _dot(x1.astype(BF16), wq_ref[...]).astype(BF16)
    heads = []
    for h in range(XA_HEADS):
        sl = slice(h * XA_HEAD_DIM, (h + 1) * XA_HEAD_DIM)
        s = _dot(q[:, sl], kt_ref[0, sl, :])
        p = jnp.exp(s - jnp.max(s, axis=-1, keepdims=True))
        o = _dot(p.astype(BF16), v_ref[0, :, sl]) / jnp.sum(p, axis=-1, keepdims=True)
        heads.append(o.astype(BF16))
    o_all = jnp.concatenate(heads, axis=1)
    half = o_all.shape[0] // 2
    for r0 in (0, half):
        rows = slice(r0, r0 + half)
        y = _dot(o_all[rows, :], wo_ref[...])
        out = _layer_norm(DN_ALPHA * x1[rows, :] + y, g1_ref[...], b1_ref[...])
        of_ref[rows, :] = out
        ob_ref[rows, :] = out.astype(BF16)


def _mixer_out_xattn(h, w_mix, b_mix, xf, g0, b0, memb, w_q, w_kv, w_out, g1, b1, batch, seq):
    kt, v = _xa_kv(memb, w_kv)
    n_mem = memb.shape[1]
    k_mix = h.shape[1]
    tm = _row_tile(seq, 1024)
    nt = seq // tm
    wq = (w_q * (XA_HEAD_DIM ** -0.5)).astype(BF16)
    vec = lambda a: a.reshape(1, D_MODEL)
    row = lambda bi, i: (bi * nt + i, 0)
    fixed = lambda bi, i: (0, 0)
    resident = pl.Buffered(1)
    return pl.pallas_call(
        _xattn_kernel,
        grid=(batch, nt),
        in_specs=[pl.BlockSpec((tm, k_mix), row),
                  pl.BlockSpec((k_mix, D_MODEL), fixed, pipeline_mode=resident),
                  pl.BlockSpec((1, D_MODEL), fixed), pl.BlockSpec((tm, D_MODEL), row),
                  pl.BlockSpec((1, D_MODEL), fixed), pl.BlockSpec((1, D_MODEL), fixed),
                  pl.BlockSpec((D_MODEL, D_MODEL), fixed, pipeline_mode=resident),
                  pl.BlockSpec((1, D_MODEL, n_mem), lambda bi, i: (bi, 0, 0)),
                  pl.BlockSpec((1, n_mem, D_MODEL), lambda bi, i: (bi, 0, 0)),
                  pl.BlockSpec((D_MODEL, D_MODEL), fixed, pipeline_mode=resident),
                  pl.BlockSpec((1, D_MODEL), fixed), pl.BlockSpec((1, D_MODEL), fixed)],
        out_specs=[pl.BlockSpec((tm, D_MODEL), row), pl.BlockSpec((tm, D_MODEL), row)],
        out_shape=[jax.ShapeDtypeStruct((batch * seq, D_MODEL), F32),
                   jax.ShapeDtypeStruct((batch * seq, D_MODEL), BF16)],
        compiler_params=_params(2),
        name="mixer_out_xattn",
    )(h, w_mix.astype(BF16), vec(b_mix), xf, vec(g0), vec(b0), wq, kt, v, w_out.astype(BF16),
      vec(g1), vec(b1))


FF_CHUNK = 256
FF_ROW_BLOCKS = 2


def _swiglu_kernel(xb_ref, xf_ref, wg_ref, wu_ref, wo_ref, g_ref, b_ref, of_ref, ob_ref, h_ref):
    xb = xb_ref[...]
    for c0 in range(0, D_FF, FF_CHUNK):
        gate = _dot(xb, wg_ref[:, c0:c0 + FF_CHUNK])
        up = _dot(xb, wu_ref[:, c0:c0 + FF_CHUNK])
        h_ref[:, c0:c0 + FF_CHUNK] = (_silu(gate) * up).astype(BF16)
    block = h_ref.shape[0] // FF_ROW_BLOCKS
    for r0 in range(0, h_ref.shape[0], block):
        rows = slice(r0, r0 + block)
        y = _dot(h_ref[rows, :], wo_ref[...])
        out = _layer_norm(DN_ALPHA * xf_ref[rows, :] + y, g_ref[...], b_ref[...])
        of_ref[rows, :] = out
        ob_ref[rows, :] = out.astype(BF16)


def _swiglu(xf, xb, w_in, w_out, g, b):
    t = xf.shape[0]
    tm = _row_tile(t, 512)
    row = lambda i: (i, 0)
    fixed = lambda i: (0, 0)
    resident = pl.Buffered(1)
    w_in_b = w_in.astype(BF16)
    return pl.pallas_call(
        _swiglu_kernel,
        grid=(t // tm,),
        in_specs=[pl.BlockSpec((tm, D_MODEL), row), pl.BlockSpec((tm, D_MODEL), row),
                  pl.BlockSpec((D_MODEL, D_FF), lambda i: (0, 0), pipeline_mode=resident),
                  pl.BlockSpec((D_MODEL, D_FF), lambda i: (0, 1), pipeline_mode=resident),
                  pl.BlockSpec((D_FF, D_MODEL), fixed, pipeline_mode=resident),
                  pl.BlockSpec((1, D_MODEL), fixed), pl.BlockSpec((1, D_MODEL), fixed)],
        out_specs=[pl.BlockSpec((tm, D_MODEL), row), pl.BlockSpec((tm, D_MODEL), row)],
        out_shape=[jax.ShapeDtypeStruct((t, D_MODEL), F32),
                   jax.ShapeDtypeStruct((t, D_MODEL), BF16)],
        scratch_shapes=[pltpu.VMEM((tm, D_FF), BF16)],
        compiler_params=_params(1),
        name="swiglu",
    )(xb, xf, w_in_b, w_in_b, w_out.astype(BF16), g.reshape(1, D_MODEL), b.reshape(1, D_MODEL))


def kernel(x, mem, positions, cv_w_in, cv_b_in, cv_w_dw, cv_b_dw, cv_ln_g, cv_ln_b, cv_w_out, cv_b_out, mb_w_in, mb_w_conv, mb_b_conv, mb_dt_bias, mb_a_log, mb_d, mb_norm_g, mb_w_out, da_w_qkv, da_lq1, da_lk1, da_lq2, da_lk2, da_subln_g, da_w_out, xa_w_q, xa_w_kv, xa_w_out, ff_w_in, ff_w_out, ln_g, ln_b):
    batch, seq, _ = x.shape
    t = batch * seq
    xf = x.reshape(t, D_MODEL)
    xb = None
    memb = mem.astype(BF16)
    pos = positions.reshape(t, 1).astype(F32)
    no_bias = jnp.zeros((D_MODEL,), F32)
    for i in range(DEPTH):
        mixer, j = i % N_MIXERS, i // N_MIXERS
        if mixer == 0:
            u = _glu(xf if xb is None else xb, cv_w_in[j].astype(BF16), cv_b_in[j])
            h = _dwconv(u, cv_w_dw[j], cv_b_dw[j], cv_ln_g[j], cv_ln_b[j], batch, seq)
            w_mix, b_mix = cv_w_out[j], cv_b_out[j]
        elif mixer == 1:
            h = _ssd_mixer(xb, mb_w_in[j], mb_w_conv[j], mb_b_conv[j], mb_dt_bias[j],
                           mb_a_log[j], mb_d[j], mb_norm_g[j], batch, seq)
            w_mix, b_mix = mb_w_out[j], no_bias
        else:
            lambda_init = 0.8 - 0.6 * math.exp(-0.3 * i)
            qkv = _qkv_rope(xb, da_w_qkv[j], pos)
            h = _diff_attn(qkv, da_lq1[j], da_lk1[j], da_lq2[j], da_lk2[j], da_subln_g[j],
                           lambda_init, batch, seq)
            w_mix, b_mix = da_w_out[j], no_bias
        xf, xb = _mixer_out_xattn(h, w_mix, b_mix, xf, ln_g[i, 0], ln_b[i, 0], memb, xa_w_q[i],
                                  xa_w_kv[i], xa_w_out[i], ln_g[i, 1], ln_b[i, 1], batch, seq)
        xf, xb = _swiglu(xf, xb, ff_w_in[i], ff_w_out[i], ln_g[i, 2], ln_b[i, 2])
    return xf.reshape(batch, seq, D_MODEL)
```
